```python
import math
import jax
import jax.numpy as jnp
from jax import lax
import numpy as np

D_MODEL = 1024
BATCH = 8
SEQ = 8192
DEPTH = 2

HEAD_DIM = 64
ATTN_SCALE = HEAD_DIM ** -0.5
SWA_HEADS = D_MODEL // 128
SWA_KV_HEADS = SWA_HEADS // 4
SWA_GROUP = SWA_HEADS // SWA_KV_HEADS
SWA_WINDOW = 128
SWA_BLOCK = 128
SC_WIDTH = D_MODEL // 2
SC_KSIZE = 3
MOBA_HEADS = D_MODEL // 128
MOBA_BLOCK = 256
MOBA_TOPK = 3
MOBA_QCHUNK = 32
N_ALIBI_HEADS = SWA_HEADS + MOBA_HEADS
FFN_HIDDEN = -(-8 * D_MODEL // (3 * 256)) * 256
ALPHA = (2 * DEPTH) ** 0.25
BETA = (8 * DEPTH) ** -0.25
LN_EPS = 1e-5

A_Q = SWA_HEADS * HEAD_DIM
A_KV = SWA_KV_HEADS * HEAD_DIM
C_QKV = MOBA_HEADS * HEAD_DIM
PROJ_SIZES = (A_Q, A_KV, A_KV,
              SC_WIDTH, SC_WIDTH, SC_WIDTH,
              C_QKV, C_QKV, C_QKV,
              D_MODEL, D_MODEL, D_MODEL)
PROJ_SPLITS = tuple(int(s) for s in np.cumsum(PROJ_SIZES)[:-1])
PROJ_WIDTH = int(sum(PROJ_SIZES))

kernel_name = 'hybrid_swa_shortconv_moba_deepnorm'


def alibi_slopes():
    i = jnp.arange(N_ALIBI_HEADS, dtype=jnp.float32)
    s = jnp.exp2(-8.0 * (i + 1.0) / N_ALIBI_HEADS)
    return s[:SWA_HEADS], s[SWA_HEADS:]


def layer_norm(x, g, b):
    xf = x.astype(jnp.float32)
    mu = jnp.mean(xf, axis=-1, keepdims=True)
    var = jnp.mean(jnp.square(xf - mu), axis=-1, keepdims=True)
    y = (xf - mu) * lax.rsqrt(var + LN_EPS) * g.astype(jnp.float32) + b.astype(jnp.float32)
    return y.astype(x.dtype)


def sliding_window_attention(q, k, v, sinks, slopes):
    B, S = q.shape[0], q.shape[1]
    nb = S // SWA_BLOCK
    qb = q.reshape(B, nb, SWA_BLOCK, SWA_KV_HEADS, SWA_GROUP, HEAD_DIM).astype(jnp.float32)
    kb = k.reshape(B, nb, SWA_BLOCK, SWA_KV_HEADS, HEAD_DIM)
    vb = v.reshape(B, nb, SWA_BLOCK, SWA_KV_HEADS, HEAD_DIM)
    shift = ((0, 0), (1, 0), (0, 0), (0, 0), (0, 0))
    kw = jnp.concatenate([jnp.pad(kb, shift)[:, :-1], kb], axis=2)
    vw = jnp.concatenate([jnp.pad(vb, shift)[:, :-1], vb], axis=2)
    logits = jnp.einsum('bnqhgd,bnkhd->bnhgqk', qb, kw.astype(jnp.float32)) * ATTN_SCALE
    blk = jnp.arange(nb)[:, None] * SWA_BLOCK
    qpos = blk + jnp.arange(SWA_BLOCK)[None, :]
    kpos = blk - SWA_BLOCK + jnp.arange(2 * SWA_BLOCK)[None, :]
    dist = qpos[:, :, None] - kpos[:, None, :]
    allowed = (dist >= 0) & (dist < SWA_WINDOW) & (kpos[:, None, :] >= 0)
    sl = slopes.reshape(SWA_KV_HEADS, SWA_GROUP)[:, :, None, None]
    logits = logits - sl * dist[:, None, None].astype(jnp.float32)
    logits = jnp.where(allowed[:, None, None], logits, -jnp.inf)
    sink = sinks.astype(jnp.float32).reshape(SWA_KV_HEADS, SWA_GROUP)[:, :, None, None]
    m = jnp.maximum(jnp.max(logits, axis=-1, keepdims=True), sink)
    p = jnp.exp(logits - m)
    denom = jnp.sum(p, axis=-1, keepdims=True) + jnp.exp(sink - m)
    out = jnp.einsum('bnhgqk,bnkhd->bnqhgd', p / denom, vw.astype(jnp.float32))
    return out.reshape(B, S, SWA_HEADS * HEAD_DIM).astype(q.dtype)


def gated_short_conv(h, gate_b, gate_c, conv_w):
    u = gate_c * h
    up = jnp.pad(u, ((0, 0), (SC_KSIZE - 1, 0), (0, 0)))
    S = h.shape[1]
    conv = up[:, 0:S] * conv_w[0] + up[:, 1:S + 1] * conv_w[1] + up[:, 2:S + 2] * conv_w[2]
    return gate_b * conv


def moba_attention(q, k, v, slopes):
    B, S = q.shape[0], q.shape[1]
    sp = -(-S // MOBA_BLOCK) * MOBA_BLOCK
    pad = ((0, 0), (0, sp - S), (0, 0), (0, 0))
    q, k, v = [jnp.pad(t, pad).transpose(0, 2, 1, 3) for t in (q, k, v)]
    nblk = sp // MOBA_BLOCK
    top = min(MOBA_TOPK, nblk)
    kbl = k.reshape(B, MOBA_HEADS, nblk, MOBA_BLOCK, HEAD_DIM)
    vbl = v.reshape(B, MOBA_HEADS, nblk, MOBA_BLOCK, HEAD_DIM)
    kmean = jnp.mean(kbl.astype(jnp.float32), axis=3)
    gate = jnp.einsum('bhsd,bhnd->bhsn', q.astype(jnp.float32), kmean)
    cur = jnp.arange(sp) // MOBA_BLOCK
    past = jnp.arange(nblk)[None, :] < cur[:, None]
    gate = jnp.where(past, gate, -jnp.inf)
    _, idx = lax.top_k(gate, top)
    nch = sp // MOBA_QCHUNK
    q_ch = q.reshape(B, MOBA_HEADS, nch, MOBA_QCHUNK, HEAD_DIM).transpose(2, 0, 1, 3, 4)
    idx_ch = idx.reshape(B, MOBA_HEADS, nch, MOBA_QCHUNK, top).transpose(2, 0, 1, 3, 4)
    bi = jnp.arange(B)[:, None, None, None]
    hi = jnp.arange(MOBA_HEADS)[None, :, None, None]
    sl4 = slopes[None, :, None, None]
    sl5 = slopes[None, :, None, None, None]
    koff = jnp.arange(MOBA_BLOCK)

    def chunk(args):
        c, qc, ic = args
        qpos = c * MOBA_QCHUNK + jnp.arange(MOBA_QCHUNK)
        own = (c * MOBA_QCHUNK) // MOBA_BLOCK
        qf = qc.astype(jnp.float32)
        kg = kbl[bi, hi, ic].astype(jnp.float32)
        vg = vbl[bi, hi, ic].astype(jnp.float32)
        ls = jnp.einsum('bhqd,bhqjkd->bhqjk', qf, kg) * ATTN_SCALE
        kpos_sel = ic[..., None] * MOBA_BLOCK + koff
        dist_sel = (qpos[None, None, :, None, None] - kpos_sel).astype(jnp.float32)
        valid = jnp.arange(top)[None, :] < (qpos // MOBA_BLOCK)[:, None]
        ls = jnp.where(valid[None, None, :, :, None], ls - sl5 * dist_sel, -jnp.inf)
        kown = lax.dynamic_index_in_dim(kbl, own, axis=2, keepdims=False).astype(jnp.float32)
        vown = lax.dynamic_index_in_dim(vbl, own, axis=2, keepdims=False).astype(jnp.float32)
        lo = jnp.einsum('bhqd,bhkd->bhqk', qf, kown) * ATTN_SCALE
        dist_own = qpos[:, None] - (own * MOBA_BLOCK + koff)[None, :]
        lo = jnp.where(dist_own >= 0, lo - sl4 * dist_own.astype(jnp.float32), -jnp.inf)
        logits = jnp.concatenate([ls.reshape(B, MOBA_HEADS, MOBA_QCHUNK, top * MOBA_BLOCK), lo], axis=-1)
        p = jax.nn.softmax(logits, axis=-1)
        ps = p[..., :top * MOBA_BLOCK].reshape(B, MOBA_HEADS, MOBA_QCHUNK, top, MOBA_BLOCK)
        po = p[..., top * MOBA_BLOCK:]
        return (jnp.einsum('bhqjk,bhqjkd->bhqd', ps, vg)
                + jnp.einsum('bhqk,bhkd->bhqd', po, vown))

    out = lax.map(chunk, (jnp.arange(nch), q_ch, idx_ch))
    out = out.transpose(1, 0, 3, 2, 4).reshape(B, sp, MOBA_HEADS * HEAD_DIM)[:, :S]
    return out.astype(q.dtype)


def hybrid_mixer(x, w_in, sinks, conv_w, w_br_a, w_br_b, w_br_c, w_out):
    B, S = x.shape[0], x.shape[1]
    proj = jnp.einsum('bsd,de->bse', x, w_in)
    a_q, a_k, a_v, b_h, b_b, b_c, c_q, c_k, c_v, g_a, g_b, g_c = jnp.split(proj, PROJ_SPLITS, axis=-1)
    slopes_a, slopes_c = alibi_slopes()
    y_a = sliding_window_attention(
        a_q.reshape(B, S, SWA_HEADS, HEAD_DIM),
        a_k.reshape(B, S, SWA_KV_HEADS, HEAD_DIM),
        a_v.reshape(B, S, SWA_KV_HEADS, HEAD_DIM), sinks, slopes_a)
    y_b = gated_short_conv(b_h, b_b, b_c, conv_w)
    y_c = moba_attention(
        c_q.reshape(B, S, MOBA_HEADS, HEAD_DIM),
        c_k.reshape(B, S, MOBA_HEADS, HEAD_DIM),
        c_v.reshape(B, S, MOBA_HEADS, HEAD_DIM), slopes_c)
    merged = (jax.nn.sigmoid(g_a) * (y_a @ w_br_a)
              + jax.nn.sigmoid(g_b) * (y_b @ w_br_b)
              + jax.nn.sigmoid(g_c) * (y_c @ w_br_c))
    return merged @ w_out


def swiglu(x, w_gate, w_up, w_down):
    return (jax.nn.silu(x @ w_gate) * (x @ w_up)) @ w_down


def setup_inputs(seed: int = 0) -> dict:
    key = jax.random.key(seed)
    ks = jax.random.split(key, 16)
    f32 = jnp.float32
    nrm = lambda k, shape, s: jax.random.normal(k, shape, f32) * s
    L = DEPTH
    br_in = SWA_HEADS * HEAD_DIM
    return {
        'x': jax.random.normal(ks[0], (BATCH, SEQ, D_MODEL), f32),
        'w_in': nrm(ks[1], (L, D_MODEL, PROJ_WIDTH), D_MODEL ** -0.5),
        'attn_sinks': nrm(ks[2], (L, SWA_HEADS), 0.5),
        'conv_w': nrm(ks[3], (L, SC_KSIZE, SC_WIDTH), SC_KSIZE ** -0.5),
        'w_branch_a': nrm(ks[4], (L, br_in, D_MODEL), br_in ** -0.5 * BETA),
        'w_branch_b': nrm(ks[5], (L, SC_WIDTH, D_MODEL), SC_WIDTH ** -0.5 * BETA),
        'w_branch_c': nrm(ks[6], (L, MOBA_HEADS * HEAD_DIM, D_MODEL), (MOBA_HEADS * HEAD_DIM) ** -0.5 * BETA),
        'w_out': nrm(ks[7], (L, D_MODEL, D_MODEL), D_MODEL ** -0.5 * BETA),
        'ln1_g': 1.0 + nrm(ks[8], (L, D_MODEL), 0.02),
        'ln1_b': nrm(ks[9], (L, D_MODEL), 0.02),
        'w_ffn_gate': nrm(ks[10], (L, D_MODEL, FFN_HIDDEN), D_MODEL ** -0.5 * BETA),
        'w_ffn_up': nrm(ks[11], (L, D_MODEL, FFN_HIDDEN), D_MODEL ** -0.5 * BETA),
        'w_ffn_down': nrm(ks[12], (L, FFN_HIDDEN, D_MODEL), FFN_HIDDEN ** -0.5 * BETA),
        'ln2_g': 1.0 + nrm(ks[13], (L, D_MODEL), 0.02),
        'ln2_b': nrm(ks[14], (L, D_MODEL), 0.02),
    }


def reference(x, w_in, attn_sinks, conv_w, w_branch_a, w_branch_b, w_branch_c, w_out,
              ln1_g, ln1_b, w_ffn_gate, w_ffn_up, w_ffn_down, ln2_g, ln2_b):
    for l in range(DEPTH):
        mix = hybrid_mixer(x, w_in[l], attn_sinks[l], conv_w[l], w_branch_a[l],
                           w_branch_b[l], w_branch_c[l], w_out[l])
        x = layer_norm(ALPHA * x + mix, ln1_g[l], ln1_b[l])
        ffn = swiglu(x, w_ffn_gate[l], w_ffn_up[l], w_ffn_down[l])
        x = layer_norm(ALPHA * x + ffn, ln2_g[l], ln2_b[l])
    return x
```

```python
import functools

import jax
import jax.numpy as jnp
import numpy as np
from jax import lax
from jax.experimental import pallas as pl
from jax.experimental.pallas import tpu as pltpu

D_MODEL = 1024
HEAD_DIM = 64
ATTN_SCALE = HEAD_DIM ** -0.5
SWA_HEADS = 8
SWA_KV_HEADS = 2
SWA_GROUP = SWA_HEADS // SWA_KV_HEADS
SWA_WINDOW = 128
SC_WIDTH = 512
MOBA_HEADS = 8
MOBA_BLOCK = 256
MOBA_TOPK = 3
N_ALIBI_HEADS = SWA_HEADS + MOBA_HEADS
FFN_HIDDEN = 2816
DEPTH = 2
ALPHA = (2 * DEPTH) ** 0.25
LN_EPS = 1e-5

A_Q = SWA_HEADS * HEAD_DIM
A_KV = SWA_KV_HEADS * HEAD_DIM
C_QKV = MOBA_HEADS * HEAD_DIM
QK_A_W = A_Q + A_KV
QK_C_W = 2 * C_QKV
HBC_W = 3 * SC_WIDTH
GATES_W = 3 * D_MODEL
VT_ROWS = C_QKV + A_KV
W_MAIN_W = QK_A_W + QK_C_W + HBC_W + GATES_W

LANES = 128
PAIR = 2 * HEAD_DIM
BLK = MOBA_BLOCK
HALO = 8

ROW_TILE = 512
VMEM_LIMIT = 56 * 1024 * 1024

NEG_INF = float("-inf")

SWA_HEAD_ORDER = tuple(kvh * SWA_GROUP + g for g in range(SWA_GROUP) for kvh in range(SWA_KV_HEADS))


def _dot(a, b):
    return jnp.dot(a, b, preferred_element_type=jnp.float32)


def _dot_nt(a, b):
    return lax.dot_general(a, b, (((1,), (1,)), ((), ())), preferred_element_type=jnp.float32)


def _layer_norm(y, g, b):
    mu = jnp.mean(y, axis=-1, keepdims=True)
    d = y - mu
    var = jnp.mean(d * d, axis=-1, keepdims=True)
    return d * lax.rsqrt(var + LN_EPS) * g + b


def _half_mask(shape, half):
    lane = lax.broadcasted_iota(jnp.int32, shape, 1)
    return (lane >= HEAD_DIM) if half else (lane < HEAD_DIM)


def _proj_kernel(x_ref, w_ref, wvt_ref, qka_ref, qkc_ref, hbc_ref, gates_ref, vt_ref):
    xb = x_ref[...].astype(jnp.bfloat16)
    off = 0
    for ref, width in ((qka_ref, QK_A_W), (qkc_ref, QK_C_W), (hbc_ref, HBC_W), (gates_ref, GATES_W)):
        ref[...] = _dot(xb, w_ref[:, off:off + width]).astype(ref.dtype)
        off += width
    for s in range(ROW_TILE // BLK):
        vt_ref[s] = _dot_nt(wvt_ref[...], xb[s * BLK:(s + 1) * BLK]).astype(vt_ref.dtype)


def _project(x2d, w_main, w_vt):
    n = x2d.shape[0]
    const = lambda i: (0, 0)
    row = lambda i: (i, 0)
    bf = jnp.bfloat16
    return pl.pallas_call(
        _proj_kernel,
        grid=(n // ROW_TILE,),
        in_specs=[
            pl.BlockSpec((ROW_TILE, D_MODEL), row),
            pl.BlockSpec((D_MODEL, W_MAIN_W), const, pipeline_mode=pl.Buffered(1)),
            pl.BlockSpec((VT_ROWS, D_MODEL), const, pipeline_mode=pl.Buffered(1)),
        ],
        out_specs=[
            pl.BlockSpec((ROW_TILE, QK_A_W), row),
            pl.BlockSpec((ROW_TILE, QK_C_W), row),
            pl.BlockSpec((ROW_TILE, HBC_W), row),
            pl.BlockSpec((ROW_TILE, GATES_W), row),
            pl.BlockSpec((ROW_TILE // BLK, VT_ROWS, BLK), lambda i: (i, 0, 0)),
        ],
        out_shape=[
            jax.ShapeDtypeStruct((n, QK_A_W), bf),
            jax.ShapeDtypeStruct((n, QK_C_W), bf),
            jax.ShapeDtypeStruct((n, HBC_W), bf),
            jax.ShapeDtypeStruct((n, GATES_W), bf),
            jax.ShapeDtypeStruct((n // BLK, VT_ROWS, BLK), bf),
        ],
        compiler_params=pltpu.CompilerParams(
            dimension_semantics=("arbitrary",), vmem_limit_bytes=VMEM_LIMIT),
        name="in_proj",
    )(x2d, w_main, w_vt)


def _swa_kernel(slopes_ref, sinks_ref, q_ref, kc_ref, kp_ref, vc_ref, vp_ref, o_ref):
    i = pl.program_id(1)
    ki = lax.broadcasted_iota(jnp.int32, (BLK, BLK), 0)
    qi = lax.broadcasted_iota(jnp.int32, (BLK, BLK), 1)
    dist_c = qi - ki
    ok_c = (dist_c >= 0) & (dist_c < SWA_WINDOW)
    dist_c = dist_c.astype(jnp.float32)
    kpi = lax.broadcasted_iota(jnp.int32, (SWA_WINDOW, BLK), 0)
    qpi = lax.broadcasted_iota(jnp.int32, (SWA_WINDOW, BLK), 1)
    dist_p = qpi + SWA_WINDOW - kpi
    ok_p = (dist_p < SWA_WINDOW) & (i > 0)
    dist_p = dist_p.astype(jnp.float32)

    k_cur = kc_ref[...]
    k_prev = kp_ref[BLK - SWA_WINDOW:, :]
    v_cur = vc_ref[0]
    v_prev = vp_ref[0][:, BLK - SWA_WINDOW:]

    for g in range(SWA_GROUP):
        q_pair = q_ref[:, g * PAIR:(g + 1) * PAIR]
        outs = []
        for kvh in range(SWA_KV_HEADS):
            h = kvh * SWA_GROUP + g
            slope = slopes_ref[h]
            sink = sinks_ref[h]
            qz = jnp.where(_half_mask(q_pair.shape, kvh), q_pair, jnp.zeros_like(q_pair))
            s_c = _dot_nt(k_cur, qz) - slope * dist_c
            s_c = jnp.where(ok_c, s_c, NEG_INF)
            s_p = _dot_nt(k_prev, qz) - slope * dist_p
            s_p = jnp.where(ok_p, s_p, NEG_INF)
            m = jnp.maximum(jnp.max(s_c, axis=0, keepdims=True), jnp.max(s_p, axis=0, keepdims=True))
            m = jnp.maximum(m, sink)
            p_c = jnp.exp(s_c - m)
            p_p = jnp.exp(s_p - m)
            denom = (jnp.sum(p_c, axis=0, keepdims=True) + jnp.sum(p_p, axis=0, keepdims=True)
                     + jnp.exp(sink - m))
            vh_c = v_cur[kvh * HEAD_DIM:(kvh + 1) * HEAD_DIM]
            vh_p = v_prev[kvh * HEAD_DIM:(kvh + 1) * HEAD_DIM]
            acc = _dot(vh_c, p_c.astype(jnp.bfloat16)) + _dot(vh_p, p_p.astype(jnp.bfloat16))
            outs.append(acc / denom)
        o_ref[:, g * PAIR:(g + 1) * PAIR] = jnp.concatenate(outs, axis=0).T.astype(o_ref.dtype)


def _swa(qk_a, vt, slopes, sinks, batch, seq):
    nblk = seq // BLK
    smem = pl.BlockSpec(memory_space=pltpu.SMEM)
    k_col = A_Q // A_KV
    v_row = C_QKV // PAIR
    cur = lambda b, i: (b * nblk + i, 0)
    prev_blk = lambda b, i: b * nblk + jnp.maximum(i - 1, 0)
    return pl.pallas_call(
        _swa_kernel,
        grid=(batch, nblk),
        in_specs=[
            smem, smem,
            pl.BlockSpec((BLK, A_Q), cur),
            pl.BlockSpec((BLK, A_KV), lambda b, i: (b * nblk + i, k_col)),
            pl.BlockSpec((BLK, A_KV), lambda b, i: (prev_blk(b, i), k_col)),
            pl.BlockSpec((1, A_KV, BLK), lambda b, i: (b * nblk + i, v_row, 0)),
            pl.BlockSpec((1, A_KV, BLK), lambda b, i: (prev_blk(b, i), v_row, 0)),
        ],
        out_specs=pl.BlockSpec((BLK, A_Q), cur),
        out_shape=jax.ShapeDtypeStruct((batch * seq, A_Q), jnp.bfloat16),
        compiler_params=pltpu.CompilerParams(
            dimension_semantics=("arbitrary", "arbitrary"), vmem_limit_bytes=VMEM_LIMIT),
        name="swa",
    )(slopes, sinks, qk_a, qk_a, qk_a, vt, vt)


def _moba_kernel(slopes_ref, q_ref, k_ref, vt_ref, o_ref, kmean_ref, bias_ref, sel_ref):
    hp = pl.program_id(1)
    n = pl.program_id(2)
    nblk = kmean_ref.shape[0]
    ki = lax.broadcasted_iota(jnp.int32, (BLK, BLK), 0)
    qi = lax.broadcasted_iota(jnp.int32, (BLK, BLK), 1)

    @pl.when(n == 0)
    def _():
        kf = k_ref[...].astype(jnp.float32).reshape(nblk, BLK, PAIR)
        kmean_ref[...] = jnp.sum(kf, axis=1) * (1.0 / BLK)
        dist = (qi - ki).astype(jnp.float32)
        for hh in range(2):
            bias_ref[hh] = -slopes_ref[2 * hp + hh] * dist

    q_pair = q_ref[...]
    kmean = kmean_ref[...]
    kmean_hi = kmean.astype(jnp.bfloat16)
    kmean_lo = (kmean - kmean_hi.astype(jnp.float32)).astype(jnp.bfloat16)
    jidx = lax.broadcasted_iota(jnp.int32, (nblk, BLK), 0)
    k_own = k_ref[pl.ds(pl.multiple_of(n * BLK, BLK), BLK), :]
    vt_own = vt_ref[n]

    qzs, state = [], []
    for hh in range(2):
        qz = jnp.where(_half_mask(q_pair.shape, hh), q_pair, jnp.zeros_like(q_pair))
        qzs.append(qz)
        gate = _dot_nt(kmean_hi, qz) + _dot_nt(kmean_lo, qz)
        gate = jnp.where(jidx < n, gate, NEG_INF)
        sel = jnp.zeros((nblk, BLK), jnp.float32)
        for _ in range(MOBA_TOPK):
            gmax = jnp.max(gate, axis=0, keepdims=True)
            first = jnp.min(jnp.where(gate == gmax, jidx, nblk), axis=0, keepdims=True)
            pick = (jidx == first) & (gmax > NEG_INF)
            sel = jnp.where(pick, 1.0, sel)
            gate = jnp.where(pick, NEG_INF, gate)
        sel_ref[hh] = sel
        s = _dot_nt(k_own, qz) + bias_ref[hh]
        s = jnp.where(ki <= qi, s, NEG_INF)
        m = jnp.max(s, axis=0, keepdims=True)
        p = jnp.exp(s - m)
        l = jnp.sum(p, axis=0, keepdims=True)
        acc = _dot(vt_own[hh * HEAD_DIM:(hh + 1) * HEAD_DIM], p.astype(jnp.bfloat16))
        state += [m, l, acc]

    def body(j, carry):
        k_j = k_ref[pl.ds(pl.multiple_of(j * BLK, BLK), BLK), :]
        vt_j = vt_ref[j]
        shift_blocks = ((n - j) * BLK).astype(jnp.float32)
        new = []
        for hh in range(2):
            m, l, acc = carry[3 * hh:3 * hh + 3]
            shift = slopes_ref[2 * hp + hh] * shift_blocks
            s = _dot_nt(k_j, qzs[hh]) + bias_ref[hh]
            s = jnp.where(sel_ref[hh, pl.ds(j, 1), :] > 0.0, s, NEG_INF)
            m_new = jnp.maximum(m, jnp.max(s, axis=0, keepdims=True) - shift)
            a = jnp.exp(m - m_new)
            p = jnp.exp(s - (m_new + shift))
            l = a * l + jnp.sum(p, axis=0, keepdims=True)
            acc = a * acc + _dot(vt_j[hh * HEAD_DIM:(hh + 1) * HEAD_DIM], p.astype(jnp.bfloat16))
            new += [m_new, l, acc]
        return tuple(new)

    state = lax.fori_loop(0, n, body, tuple(state))
    outs = [state[3 * hh + 2] / state[3 * hh + 1] for hh in range(2)]
    o_ref[...] = jnp.concatenate(outs, axis=0).T.astype(o_ref.dtype)


def _moba(qk_c, vt, slopes, batch, seq):
    nblk = seq // BLK
    npair = C_QKV // PAIR
    smem = pl.BlockSpec(memory_space=pltpu.SMEM)
    return pl.pallas_call(
        _moba_kernel,
        grid=(batch, npair, nblk),
        in_specs=[
            smem,
            pl.BlockSpec((BLK, PAIR), lambda b, hp, n: (b * nblk + n, hp)),
            pl.BlockSpec((seq, PAIR), lambda b, hp, n: (b, npair + hp)),
            pl.BlockSpec((nblk, PAIR, BLK), lambda b, hp, n: (b, hp, 0)),
        ],
        out_specs=pl.BlockSpec((BLK, PAIR), lambda b, hp, n: (b * nblk + n, hp)),
        out_shape=jax.ShapeDtypeStruct((batch * seq, C_QKV), jnp.bfloat16),
        scratch_shapes=[
            pltpu.VMEM((nblk, PAIR), jnp.float32),
            pltpu.VMEM((2, BLK, BLK), jnp.float32),
            pltpu.VMEM((2, nblk, BLK), jnp.float32),
        ],
        compiler_params=pltpu.CompilerParams(
            dimension_semantics=("arbitrary", "arbitrary", "arbitrary"), vmem_limit_bytes=VMEM_LIMIT),
        name="moba",
    )(slopes, qk_c, qk_c, vt)


def _merge_kernel(x_ref, ya_ref, yc_ref, hbc_ref, halo_ref, gates_ref, convw_ref,
                  wa_ref, wb_ref, wc_ref, wo_ref, g_ref, b_ref, o_ref, u_ref, *, tiles_per_seq):
    i = pl.program_id(0)
    f32 = jnp.float32

    def gated_input(ref):
        return ref[:, 2 * SC_WIDTH:].astype(f32) * ref[:, :SC_WIDTH].astype(f32)

    hist = jnp.where(i % tiles_per_seq == 0, 0.0, gated_input(halo_ref))
    u_ref[:HALO] = hist
    u_ref[HALO:] = gated_input(hbc_ref)
    conv = (u_ref[HALO - 2:HALO - 2 + ROW_TILE] * convw_ref[0:1, :]
            + u_ref[HALO - 1:HALO - 1 + ROW_TILE] * convw_ref[1:2, :]
            + u_ref[HALO:] * convw_ref[2:3, :])
    y_b = hbc_ref[:, SC_WIDTH:2 * SC_WIDTH].astype(f32) * conv

    def gate(k):
        return jax.nn.sigmoid(gates_ref[:, k * D_MODEL:(k + 1) * D_MODEL].astype(f32))

    merged = (gate(0) * _dot(ya_ref[...], wa_ref[...])
              + gate(1) * _dot(y_b.astype(jnp.bfloat16), wb_ref[...])
              + gate(2) * _dot(yc_ref[...], wc_ref[...]))
    mix = _dot(merged.astype(jnp.bfloat16), wo_ref[...])
    o_ref[...] = _layer_norm(ALPHA * x_ref[...] + mix, g_ref[...], b_ref[...])


def _merge(x2d, y_a, y_c, hbc, gates, conv_w, w_a, w_b, w_c, w_o, ln_g, ln_b, seq):
    n = x2d.shape[0]
    row = lambda i: (i, 0)
    const = lambda i: (0, 0)
    halo_blocks = ROW_TILE // HALO
    weight = lambda shape: pl.BlockSpec(shape, const, pipeline_mode=pl.Buffered(1))
    return pl.pallas_call(
        functools.partial(_merge_kernel, tiles_per_seq=seq // ROW_TILE),
        grid=(n // ROW_TILE,),
        in_specs=[
            pl.BlockSpec((ROW_TILE, D_MODEL), row),
            pl.BlockSpec((ROW_TILE, A_Q), row),
            pl.BlockSpec((ROW_TILE, C_QKV), row),
            pl.BlockSpec((ROW_TILE, HBC_W), row),
            pl.BlockSpec((HALO, HBC_W), lambda i: (jnp.maximum(i * halo_blocks - 1, 0), 0)),
            pl.BlockSpec((ROW_TILE, GATES_W), row),
            weight((3, SC_WIDTH)),
            weight((A_Q, D_MODEL)), weight((SC_WIDTH, D_MODEL)), weight((C_QKV, D_MODEL)),
            weight((D_MODEL, D_MODEL)),
            weight((1, D_MODEL)), weight((1, D_MODEL)),
        ],
        out_specs=pl.BlockSpec((ROW_TILE, D_MODEL), row),
        out_shape=jax.ShapeDtypeStruct((n, D_MODEL), jnp.float32),
        scratch_shapes=[pltpu.VMEM((ROW_TILE + HALO, SC_WIDTH), jnp.float32)],
        compiler_params=pltpu.CompilerParams(
            dimension_semantics=("arbitrary",), vmem_limit_bytes=VMEM_LIMIT),
        name="merge",
    )(x2d, y_a, y_c, hbc, hbc, gates, conv_w, w_a, w_b, w_c, w_o, ln_g, ln_b)


def _ffn_kernel(x_ref, wg_ref, wu_ref, wd_ref, g_ref, b_ref, o_ref):
    x = x_ref[...]
    xb = x.astype(jnp.bfloat16)
    gate = _dot(xb, wg_ref[...])
    up = _dot(xb, wu_ref[...])
    act = (gate * jax.nn.sigmoid(gate) * up).astype(jnp.bfloat16)
    ffn = _dot(act, wd_ref[...])
    o_ref[...] = _layer_norm(ALPHA * x + ffn, g_ref[...], b_ref[...])


def _ffn(x2d, w_gate, w_up, w_down, ln_g, ln_b):
    n = x2d.shape[0]
    row = lambda i: (i, 0)
    const = lambda i: (0, 0)
    weight = lambda shape: pl.BlockSpec(shape, const, pipeline_mode=pl.Buffered(1))
    return pl.pallas_call(
        _ffn_kernel,
        grid=(n // ROW_TILE,),
        in_specs=[
            pl.BlockSpec((ROW_TILE, D_MODEL), row),
            weight((D_MODEL, FFN_HIDDEN)), weight((D_MODEL, FFN_HIDDEN)),
            weight((FFN_HIDDEN, D_MODEL)),
            weight((1, D_MODEL)), weight((1, D_MODEL)),
        ],
        out_specs=pl.BlockSpec((ROW_TILE, D_MODEL), row),
        out_shape=jax.ShapeDtypeStruct((n, D_MODEL), jnp.float32),
        compiler_params=pltpu.CompilerParams(
            dimension_semantics=("arbitrary",), vmem_limit_bytes=VMEM_LIMIT),
        name="ffn",
    )(x2d, w_gate, w_up, w_down, ln_g, ln_b)


def _alibi_slopes():
    i = jnp.arange(N_ALIBI_HEADS, dtype=jnp.float32)
    s = jnp.exp2(-8.0 * (i + 1.0) / N_ALIBI_HEADS)
    return s[:SWA_HEADS], s[SWA_HEADS:]


def _split_w_in(w_in):
    sizes = (A_Q, A_KV, A_KV, SC_WIDTH, SC_WIDTH, SC_WIDTH, C_QKV, C_QKV, C_QKV,
             D_MODEL, D_MODEL, D_MODEL)
    a_q, a_k, a_v, b_h, b_b, b_c, c_q, c_k, c_v, g_a, g_b, g_c = jnp.split(
        w_in, np.cumsum(sizes)[:-1].tolist(), axis=1)
    order = jnp.asarray(SWA_HEAD_ORDER)
    a_q = a_q.reshape(D_MODEL, SWA_HEADS, HEAD_DIM)[:, order].reshape(D_MODEL, A_Q)
    w_main = jnp.concatenate(
        [a_q * ATTN_SCALE, a_k, c_q * ATTN_SCALE, c_k, b_h, b_b, b_c, g_a, g_b, g_c], axis=1)
    w_vt = jnp.concatenate([c_v, a_v], axis=1).T
    return w_main.astype(jnp.bfloat16), w_vt.astype(jnp.bfloat16)


def kernel(x, w_in, attn_sinks, conv_w, w_branch_a, w_branch_b, w_branch_c, w_out,
           ln1_g, ln1_b, w_ffn_gate, w_ffn_up, w_ffn_down, ln2_g, ln2_b):
    batch, seq, d = x.shape
    assert d == D_MODEL and seq % ROW_TILE == 0 and seq % BLK == 0
    bf = jnp.bfloat16
    slopes_a, slopes_c = _alibi_slopes()
    order = jnp.asarray(SWA_HEAD_ORDER)
    h = x.reshape(batch * seq, d)
    for l in range(w_in.shape[0]):
        w_main, w_vt = _split_w_in(w_in[l])
        w_a = w_branch_a[l].reshape(SWA_HEADS, HEAD_DIM, d)[order].reshape(A_Q, d).astype(bf)
        qk_a, qk_c, hbc, gates, vt = _project(h, w_main, w_vt)
        y_a = _swa(qk_a, vt, slopes_a, attn_sinks[l].astype(jnp.float32), batch, seq)
        y_c = _moba(qk_c, vt, slopes_c, batch, seq)
        h = _merge(h, y_a, y_c, hbc, gates, conv_w[l], w_a, w_branch_b[l].astype(bf),
                   w_branch_c[l].astype(bf), w_out[l].astype(bf),
                   ln1_g[l].reshape(1, d), ln1_b[l].reshape(1, d), seq)
        h = _ffn(h, w_ffn_gate[l].astype(bf), w_ffn_up[l].astype(bf), w_ffn_down[l].astype(bf),
                 ln2_g[l].reshape(1, d), ln2_b[l].reshape(1, d))
    return h.reshape(batch, seq, d)
```

```python
import functools

import jax
import jax.numpy as jnp
import numpy as np
from jax import lax
from jax.experimental import pallas as pl
from jax.experimental.pallas import tpu as pltpu

D_MODEL = 1024
HEAD_DIM = 64
ATTN_SCALE = HEAD_DIM ** -0.5
SWA_HEADS = 8
SWA_KV_HEADS = 2
SWA_GROUP = SWA_HEADS // SWA_KV_HEADS
SWA_WINDOW = 128
SC_WIDTH = 512
MOBA_HEADS = 8
MOBA_BLOCK = 256
MOBA_TOPK = 3
N_ALIBI_HEADS = SWA_HEADS + MOBA_HEADS
FFN_HIDDEN = 2816
DEPTH = 2
ALPHA = (2 * DEPTH) ** 0.25
LN_EPS = 1e-5

A_Q = SWA_HEADS * HEAD_DIM
A_KV = SWA_KV_HEADS * HEAD_DIM
C_QKV = MOBA_HEADS * HEAD_DIM
QK_A_W = A_Q + A_KV
QK_C_W = 2 * C_QKV
HBC_W = 3 * SC_WIDTH
GATES_W = 3 * D_MODEL
VT_ROWS = C_QKV + A_KV
W_MAIN_W = QK_A_W + QK_C_W + HBC_W + GATES_W

LANES = 128
PAIR = 2 * HEAD_DIM
BLK = MOBA_BLOCK
HALO = 8

ROW_TILE = 512
VMEM_LIMIT = 56 * 1024 * 1024

NEG_INF = float("-inf")
LOG2E = 1.4426950408889634
MOBA_Q_SCALE = ATTN_SCALE * LOG2E
MOBA_GROUP = 2

SWA_HEAD_ORDER = tuple(kvh * SWA_GROUP + g for g in range(SWA_GROUP) for kvh in range(SWA_KV_HEADS))


def _dot(a, b):
    return jnp.dot(a, b, preferred_element_type=jnp.float32)


def _dot_nt(a, b):
    return lax.dot_general(a, b, (((1,), (1,)), ((), ())), preferred_element_type=jnp.float32)


def _layer_norm(y, g, b):
    mu = jnp.mean(y, axis=-1, keepdims=True)
    d = y - mu
    var = jnp.mean(d * d, axis=-1, keepdims=True)
    return d * lax.rsqrt(var + LN_EPS) * g + b


def _half_mask(shape, half):
    lane = lax.broadcasted_iota(jnp.int32, shape, 1)
    return (lane >= HEAD_DIM) if half else (lane < HEAD_DIM)


def _proj_kernel(x_ref, w_ref, wvt_ref, qka_ref, qkc_ref, hbc_ref, gates_ref, vt_ref):
    xb = x_ref[...].astype(jnp.bfloat16)
    off = 0
    for ref, width in ((qka_ref, QK_A_W), (qkc_ref, QK_C_W), (hbc_ref, HBC_W), (gates_ref, GATES_W)):
        ref[...] = _dot(xb, w_ref[:, off:off + width]).astype(ref.dtype)
        off += width
    for s in range(ROW_TILE // BLK):
        vt_ref[s] = _dot_nt(wvt_ref[...], xb[s * BLK:(s + 1) * BLK]).astype(vt_ref.dtype)


def _project(x2d, w_main, w_vt):
    n = x2d.shape[0]
    const = lambda i: (0, 0)
    row = lambda i: (i, 0)
    bf = jnp.bfloat16
    return pl.pallas_call(
        _proj_kernel,
        grid=(n // ROW_TILE,),
        in_specs=[
            pl.BlockSpec((ROW_TILE, D_MODEL), row),
            pl.BlockSpec((D_MODEL, W_MAIN_W), const, pipeline_mode=pl.Buffered(1)),
            pl.BlockSpec((VT_ROWS, D_MODEL), const, pipeline_mode=pl.Buffered(1)),
        ],
        out_specs=[
            pl.BlockSpec((ROW_TILE, QK_A_W), row),
            pl.BlockSpec((ROW_TILE, QK_C_W), row),
            pl.BlockSpec((ROW_TILE, HBC_W), row),
            pl.BlockSpec((ROW_TILE, GATES_W), row),
            pl.BlockSpec((ROW_TILE // BLK, VT_ROWS, BLK), lambda i: (i, 0, 0)),
        ],
        out_shape=[
            jax.ShapeDtypeStruct((n, QK_A_W), bf),
            jax.ShapeDtypeStruct((n, QK_C_W), bf),
            jax.ShapeDtypeStruct((n, HBC_W), bf),
            jax.ShapeDtypeStruct((n, GATES_W), bf),
            jax.ShapeDtypeStruct((n // BLK, VT_ROWS, BLK), bf),
        ],
        compiler_params=pltpu.CompilerParams(
            dimension_semantics=("arbitrary",), vmem_limit_bytes=VMEM_LIMIT),
        name="in_proj",
    )(x2d, w_main, w_vt)


def _swa_kernel(slopes_ref, sinks_ref, q_ref, kc_ref, kp_ref, vc_ref, vp_ref, o_ref):
    i = pl.program_id(1)
    ki = lax.broadcasted_iota(jnp.int32, (BLK, BLK), 0)
    qi = lax.broadcasted_iota(jnp.int32, (BLK, BLK), 1)
    dist_c = qi - ki
    ok_c = (dist_c >= 0) & (dist_c < SWA_WINDOW)
    dist_c = dist_c.astype(jnp.float32)
    kpi = lax.broadcasted_iota(jnp.int32, (SWA_WINDOW, BLK), 0)
    qpi = lax.broadcasted_iota(jnp.int32, (SWA_WINDOW, BLK), 1)
    dist_p = qpi + SWA_WINDOW - kpi
    ok_p = (dist_p < SWA_WINDOW) & (i > 0)
    dist_p = dist_p.astype(jnp.float32)

    k_cur = kc_ref[...]
    k_prev = kp_ref[BLK - SWA_WINDOW:, :]
    v_cur = vc_ref[0]
    v_prev = vp_ref[0][:, BLK - SWA_WINDOW:]

    for g in range(SWA_GROUP):
        q_pair = q_ref[:, g * PAIR:(g + 1) * PAIR]
        outs = []
        for kvh in range(SWA_KV_HEADS):
            h = kvh * SWA_GROUP + g
            slope = slopes_ref[h]
            sink = sinks_ref[h]
            qz = jnp.where(_half_mask(q_pair.shape, kvh), q_pair, jnp.zeros_like(q_pair))
            s_c = _dot_nt(k_cur, qz) - slope * dist_c
            s_c = jnp.where(ok_c, s_c, NEG_INF)
            s_p = _dot_nt(k_prev, qz) - slope * dist_p
            s_p = jnp.where(ok_p, s_p, NEG_INF)
            m = jnp.maximum(jnp.max(s_c, axis=0, keepdims=True), jnp.max(s_p, axis=0, keepdims=True))
            m = jnp.maximum(m, sink)
            p_c = jnp.exp(s_c - m)
            p_p = jnp.exp(s_p - m)
            denom = (jnp.sum(p_c, axis=0, keepdims=True) + jnp.sum(p_p, axis=0, keepdims=True)
                     + jnp.exp(sink - m))
            vh_c = v_cur[kvh * HEAD_DIM:(kvh + 1) * HEAD_DIM]
            vh_p = v_prev[kvh * HEAD_DIM:(kvh + 1) * HEAD_DIM]
            acc = _dot(vh_c, p_c.astype(jnp.bfloat16)) + _dot(vh_p, p_p.astype(jnp.bfloat16))
            outs.append(acc / denom)
        o_ref[:, g * PAIR:(g + 1) * PAIR] = jnp.concatenate(outs, axis=0).T.astype(o_ref.dtype)


def _swa(qk_a, vt, slopes, sinks, batch, seq):
    nblk = seq // BLK
    smem = pl.BlockSpec(memory_space=pltpu.SMEM)
    k_col = A_Q // A_KV
    v_row = C_QKV // PAIR
    cur = lambda b, i: (b * nblk + i, 0)
    prev_blk = lambda b, i: b * nblk + jnp.maximum(i - 1, 0)
    return pl.pallas_call(
        _swa_kernel,
        grid=(batch, nblk),
        in_specs=[
            smem, smem,
            pl.BlockSpec((BLK, A_Q), cur),
            pl.BlockSpec((BLK, A_KV), lambda b, i: (b * nblk + i, k_col)),
            pl.BlockSpec((BLK, A_KV), lambda b, i: (prev_blk(b, i), k_col)),
            pl.BlockSpec((1, A_KV, BLK), lambda b, i: (b * nblk + i, v_row, 0)),
            pl.BlockSpec((1, A_KV, BLK), lambda b, i: (prev_blk(b, i), v_row, 0)),
        ],
        out_specs=pl.BlockSpec((BLK, A_Q), cur),
        out_shape=jax.ShapeDtypeStruct((batch * seq, A_Q), jnp.bfloat16),
        compiler_params=pltpu.CompilerParams(
            dimension_semantics=("arbitrary", "arbitrary"), vmem_limit_bytes=VMEM_LIMIT),
        name="swa",
    )(slopes, sinks, qk_a, qk_a, qk_a, vt, vt)


def _online_softmax_step(s_ref, shifts, v_ts, m, l, acc):
    rows = [slice(i * BLK, (i + 1) * BLK) for i in range(len(shifts))]
    tile_max = [jnp.max(s_ref[r, :], axis=0, keepdims=True) - shift for r, shift in zip(rows, shifts)]
    m_new = functools.reduce(jnp.maximum, tile_max if m is None else [m] + tile_max)
    l_new, pv = None, None
    for r, shift, v_t in zip(rows, shifts, v_ts):
        p = jnp.exp2(s_ref[r, :] - (m_new + shift))
        psum = jnp.sum(p, axis=0, keepdims=True)
        d = _dot(v_t, p.astype(jnp.bfloat16))
        l_new = psum if l_new is None else l_new + psum
        pv = d if pv is None else pv + d
    if m is None:
        return m_new, l_new, pv
    a = jnp.exp2(m - m_new)
    return m_new, a * l + l_new, a * acc + pv


def _moba_kernel(slopes_ref, q_ref, k_ref, vt_ref, o_ref,
                 kmean_ref, bias_ref, sel_ref, sown_ref, sa_ref, sb_ref):
    hp = pl.program_id(1)
    n = pl.program_id(2)
    nblk = kmean_ref.shape[0]

    @pl.when(n == 0)
    def _():
        kf = k_ref[...].astype(jnp.float32).reshape(nblk, BLK, PAIR)
        kmean_ref[...] = jnp.sum(kf, axis=1) * (1.0 / BLK)
        ki = lax.broadcasted_iota(jnp.int32, (BLK, BLK), 0)
        qi = lax.broadcasted_iota(jnp.int32, (BLK, BLK), 1)
        dist = (qi - ki).astype(jnp.float32)
        for hh in range(2):
            bias = -(slopes_ref[2 * hp + hh] * LOG2E) * dist
            bias_ref[hh, 0] = bias
            bias_ref[hh, 1] = jnp.where(ki <= qi, bias, NEG_INF)

    q_pair = q_ref[...]
    kmean = kmean_ref[...]
    kmean_hi = kmean.astype(jnp.bfloat16)
    kmean_lo = (kmean - kmean_hi.astype(jnp.float32)).astype(jnp.bfloat16)
    jidx = lax.broadcasted_iota(jnp.int32, (nblk, BLK), 0)
    k_own = k_ref[pl.ds(pl.multiple_of(n * BLK, BLK), BLK), :]
    vt_own = vt_ref[n]

    qzs, state = [], []
    for hh in range(2):
        qz = jnp.where(_half_mask(q_pair.shape, hh), q_pair, jnp.zeros_like(q_pair))
        qzs.append(qz)
        gate = _dot_nt(kmean_hi, qz) + _dot_nt(kmean_lo, qz)
        gate = jnp.where(jidx < n, gate, NEG_INF)
        sel = jnp.zeros((nblk, BLK), jnp.float32)
        for _ in range(MOBA_TOPK):
            gmax = jnp.max(gate, axis=0, keepdims=True)
            first = jnp.min(jnp.where(gate == gmax, jidx, nblk), axis=0, keepdims=True)
            pick = (jidx == first) & (gmax > NEG_INF)
            sel = jnp.where(pick, 1.0, sel)
            gate = jnp.where(pick, NEG_INF, gate)
        sel_ref[hh] = sel
        sown_ref[hh] = _dot_nt(k_own, qz) + bias_ref[hh, 1]

    def group_scores(dst_ref, g):
        j0 = jnp.minimum(g, last) * MOBA_GROUP
        k_g = k_ref[pl.ds(pl.multiple_of(j0 * BLK, MOBA_GROUP * BLK), MOBA_GROUP * BLK), :]
        for hh in range(2):
            for i in range(MOBA_GROUP):
                s = _dot_nt(k_g[i * BLK:(i + 1) * BLK], qzs[hh]) + bias_ref[hh, 0]
                chosen = jnp.where(g <= last, sel_ref[hh, pl.ds(j0 + i, 1), :], 0.0)
                dst_ref[hh, i * BLK:(i + 1) * BLK, :] = jnp.where(chosen > 0.0, s, NEG_INF)

    def group_update(src_ref, g, carry):
        j0 = jnp.minimum(g, last) * MOBA_GROUP
        new = []
        for hh in range(2):
            block_shift = slopes_ref[2 * hp + hh] * (LOG2E * BLK)
            shifts = [block_shift * (n - j0 - i).astype(jnp.float32) for i in range(MOBA_GROUP)]
            v_ts = [vt_ref[j0 + i][hh * HEAD_DIM:(hh + 1) * HEAD_DIM] for i in range(MOBA_GROUP)]
            new += list(_online_softmax_step(src_ref.at[hh], shifts, v_ts, *carry[3 * hh:3 * hh + 3]))
        return tuple(new)

    n_groups = (n + MOBA_GROUP - 1) // MOBA_GROUP
    last = jnp.maximum(n_groups - 1, 0)
    group_scores(sa_ref, 0)
    for hh in range(2):
        state += list(_online_softmax_step(
            sown_ref.at[hh], [0.0], [vt_own[hh * HEAD_DIM:(hh + 1) * HEAD_DIM]], None, None, None))

    def body(t, carry):
        group_scores(sb_ref, 2 * t + 1)
        carry = group_update(sa_ref, 2 * t, carry)
        group_scores(sa_ref, 2 * t + 2)
        return group_update(sb_ref, 2 * t + 1, carry)

    state = lax.fori_loop(0, (n_groups + 1) // 2, body, tuple(state))
    outs = [state[3 * hh + 2] / state[3 * hh + 1] for hh in range(2)]
    o_ref[...] = jnp.concatenate(outs, axis=0).T.astype(o_ref.dtype)


def _moba(qk_c, vt, slopes, batch, seq):
    nblk = seq // BLK
    npair = C_QKV // PAIR
    smem = pl.BlockSpec(memory_space=pltpu.SMEM)
    return pl.pallas_call(
        _moba_kernel,
        grid=(batch, npair, nblk),
        in_specs=[
            smem,
            pl.BlockSpec((BLK, PAIR), lambda b, hp, n: (b * nblk + n, hp)),
            pl.BlockSpec((seq, PAIR), lambda b, hp, n: (b, npair + hp)),
            pl.BlockSpec((nblk, PAIR, BLK), lambda b, hp, n: (b, hp, 0)),
        ],
        out_specs=pl.BlockSpec((BLK, PAIR), lambda b, hp, n: (b * nblk + n, hp)),
        out_shape=jax.ShapeDtypeStruct((batch * seq, C_QKV), jnp.bfloat16),
        scratch_shapes=[
            pltpu.VMEM((nblk, PAIR), jnp.float32),
            pltpu.VMEM((2, 2, BLK, BLK), jnp.float32),
            pltpu.VMEM((2, nblk, BLK), jnp.float32),
            pltpu.VMEM((2, BLK, BLK), jnp.float32),
            pltpu.VMEM((2, MOBA_GROUP * BLK, BLK), jnp.float32),
            pltpu.VMEM((2, MOBA_GROUP * BLK, BLK), jnp.float32),
        ],
        compiler_params=pltpu.CompilerParams(
            dimension_semantics=("arbitrary", "arbitrary", "arbitrary"), vmem_limit_bytes=VMEM_LIMIT),
        name="moba",
    )(slopes, qk_c, qk_c, vt)


def _merge_kernel(x_ref, ya_ref, yc_ref, hbc_ref, halo_ref, gates_ref, convw_ref,
                  wa_ref, wb_ref, wc_ref, wo_ref, g_ref, b_ref, o_ref, u_ref, *, tiles_per_seq):
    i = pl.program_id(0)
    f32 = jnp.float32

    def gated_input(ref):
        return ref[:, 2 * SC_WIDTH:].astype(f32) * ref[:, :SC_WIDTH].astype(f32)

    hist = jnp.where(i % tiles_per_seq == 0, 0.0, gated_input(halo_ref))
    u_ref[:HALO] = hist
    u_ref[HALO:] = gated_input(hbc_ref)
    conv = (u_ref[HALO - 2:HALO - 2 + ROW_TILE] * convw_ref[0:1, :]
            + u_ref[HALO - 1:HALO - 1 + ROW_TILE] * convw_ref[1:2, :]
            + u_ref[HALO:] * convw_ref[2:3, :])
    y_b = hbc_ref[:, SC_WIDTH:2 * SC_WIDTH].astype(f32) * conv

    def gate(k):
        return jax.nn.sigmoid(gates_ref[:, k * D_MODEL:(k + 1) * D_MODEL].astype(f32))

    merged = (gate(0) * _dot(ya_ref[...], wa_ref[...])
              + gate(1) * _dot(y_b.astype(jnp.bfloat16), wb_ref[...])
              + gate(2) * _dot(yc_ref[...], wc_ref[...]))
    mix = _dot(merged.astype(jnp.bfloat16), wo_ref[...])
    o_ref[...] = _layer_norm(ALPHA * x_ref[...] + mix, g_ref[...], b_ref[...])


def _merge(x2d, y_a, y_c, hbc, gates, conv_w, w_a, w_b, w_c, w_o, ln_g, ln_b, seq):
    n = x2d.shape[0]
    row = lambda i: (i, 0)
    const = lambda i: (0, 0)
    halo_blocks = ROW_TILE // HALO
    weight = lambda shape: pl.BlockSpec(shape, const, pipeline_mode=pl.Buffered(1))
    return pl.pallas_call(
        functools.partial(_merge_kernel, tiles_per_seq=seq // ROW_TILE),
        grid=(n // ROW_TILE,),
        in_specs=[
            pl.BlockSpec((ROW_TILE, D_MODEL), row),
            pl.BlockSpec((ROW_TILE, A_Q), row),
            pl.BlockSpec((ROW_TILE, C_QKV), row),
            pl.BlockSpec((ROW_TILE, HBC_W), row),
            pl.BlockSpec((HALO, HBC_W), lambda i: (jnp.maximum(i * halo_blocks - 1, 0), 0)),
            pl.BlockSpec((ROW_TILE, GATES_W), row),
            weight((3, SC_WIDTH)),
            weight((A_Q, D_MODEL)), weight((SC_WIDTH, D_MODEL)), weight((C_QKV, D_MODEL)),
            weight((D_MODEL, D_MODEL)),
            weight((1, D_MODEL)), weight((1, D_MODEL)),
        ],
        out_specs=pl.BlockSpec((ROW_TILE, D_MODEL), row),
        out_shape=jax.ShapeDtypeStruct((n, D_MODEL), jnp.float32),
        scratch_shapes=[pltpu.VMEM((ROW_TILE + HALO, SC_WIDTH), jnp.float32)],
        compiler_params=pltpu.CompilerParams(
            dimension_semantics=("arbitrary",), vmem_limit_bytes=VMEM_LIMIT),
        name="merge",
    )(x2d, y_a, y_c, hbc, hbc, gates, conv_w, w_a, w_b, w_c, w_o, ln_g, ln_b)


def _ffn_kernel(x_ref, wg_ref, wu_ref, wd_ref, g_ref, b_ref, o_ref):
    x = x_ref[...]
    xb = x.astype(jnp.bfloat16)
    gate = _dot(xb, wg_ref[...])
    up = _dot(xb, wu_ref[...])
    act = (gate * jax.nn.sigmoid(gate) * up).astype(jnp.bfloat16)
    ffn = _dot(act, wd_ref[...])
    o_ref[...] = _layer_norm(ALPHA * x + ffn, g_ref[...], b_ref[...])


def _ffn(x2d, w_gate, w_up, w_down, ln_g, ln_b):
    n = x2d.shape[0]
    row = lambda i: (i, 0)
    const = lambda i: (0, 0)
    weight = lambda shape: pl.BlockSpec(shape, const, pipeline_mode=pl.Buffered(1))
    return pl.pallas_call(
        _ffn_kernel,
        grid=(n // ROW_TILE,),
        in_specs=[
            pl.BlockSpec((ROW_TILE, D_MODEL), row),
            weight((D_MODEL, FFN_HIDDEN)), weight((D_MODEL, FFN_HIDDEN)),
            weight((FFN_HIDDEN, D_MODEL)),
            weight((1, D_MODEL)), weight((1, D_MODEL)),
        ],
        out_specs=pl.BlockSpec((ROW_TILE, D_MODEL), row),
        out_shape=jax.ShapeDtypeStruct((n, D_MODEL), jnp.float32),
        compiler_params=pltpu.CompilerParams(
            dimension_semantics=("arbitrary",), vmem_limit_bytes=VMEM_LIMIT),
        name="ffn",
    )(x2d, w_gate, w_up, w_down, ln_g, ln_b)


def _alibi_slopes():
    i = jnp.arange(N_ALIBI_HEADS, dtype=jnp.float32)
    s = jnp.exp2(-8.0 * (i + 1.0) / N_ALIBI_HEADS)
    return s[:SWA_HEADS], s[SWA_HEADS:]


def _split_w_in(w_in):
    sizes = (A_Q, A_KV, A_KV, SC_WIDTH, SC_WIDTH, SC_WIDTH, C_QKV, C_QKV, C_QKV,
             D_MODEL, D_MODEL, D_MODEL)
    a_q, a_k, a_v, b_h, b_b, b_c, c_q, c_k, c_v, g_a, g_b, g_c = jnp.split(
        w_in, np.cumsum(sizes)[:-1].tolist(), axis=1)
    order = jnp.asarray(SWA_HEAD_ORDER)
    a_q = a_q.reshape(D_MODEL, SWA_HEADS, HEAD_DIM)[:, order].reshape(D_MODEL, A_Q)
    w_main = jnp.concatenate(
        [a_q * ATTN_SCALE, a_k, c_q * MOBA_Q_SCALE, c_k, b_h, b_b, b_c, g_a, g_b, g_c], axis=1)
    w_vt = jnp.concatenate([c_v, a_v], axis=1).T
    return w_main.astype(jnp.bfloat16), w_vt.astype(jnp.bfloat16)


def kernel(x, w_in, attn_sinks, conv_w, w_branch_a, w_branch_b, w_branch_c, w_out,
           ln1_g, ln1_b, w_ffn_gate, w_ffn_up, w_ffn_down, ln2_g, ln2_b):
    batch, seq, d = x.shape
    assert d == D_MODEL and seq % ROW_TILE == 0 and seq % BLK == 0
    bf = jnp.bfloat16
    slopes_a, slopes_c = _alibi_slopes()
    order = jnp.asarray(SWA_HEAD_ORDER)
    h = x.reshape(batch * seq, d)
    for l in range(w_in.shape[0]):
        w_main, w_vt = _split_w_in(w_in[l])
        w_a = w_branch_a[l].reshape(SWA_HEADS, HEAD_DIM, d)[order].reshape(A_Q, d).astype(bf)
        qk_a, qk_c, hbc, gates, vt = _project(h, w_main, w_vt)
        y_a = _swa(qk_a, vt, slopes_a, attn_sinks[l].astype(jnp.float32), batch, seq)
        y_c = _moba(qk_c, vt, slopes_c, batch, seq)
        h = _merge(h, y_a, y_c, hbc, gates, conv_w[l], w_a, w_branch_b[l].astype(bf),
                   w_branch_c[l].astype(bf), w_out[l].astype(bf),
                   ln1_g[l].reshape(1, d), ln1_b[l].reshape(1, d), seq)
        h = _ffn(h, w_ffn_gate[l].astype(bf), w_ffn_up[l].astype(bf), w_ffn_down[l].astype(bf),
                 ln2_g[l].reshape(1, d), ln2_b[l].reshape(1, d))
    return h.reshape(batch, seq, d)
```

```python
import functools

import jax
import jax.numpy as jnp
import numpy as np
from jax import lax
from jax.experimental import pallas as pl
from jax.experimental.pallas import tpu as pltpu

D_MODEL = 1024
HEAD_DIM = 64
ATTN_SCALE = HEAD_DIM ** -0.5
SWA_HEADS = 8
SWA_KV_HEADS = 2
SWA_GROUP = SWA_HEADS // SWA_KV_HEADS
SWA_WINDOW = 128
SC_WIDTH = 512
MOBA_HEADS = 8
MOBA_BLOCK = 256
MOBA_TOPK = 3
N_ALIBI_HEADS = SWA_HEADS + MOBA_HEADS
FFN_HIDDEN = 2816
DEPTH = 2
ALPHA = (2 * DEPTH) ** 0.25
LN_EPS = 1e-5

A_Q = SWA_HEADS * HEAD_DIM
A_KV = SWA_KV_HEADS * HEAD_DIM
C_QKV = MOBA_HEADS * HEAD_DIM
QK_A_W = A_Q + A_KV
QK_C_W = 2 * C_QKV
HBC_W = 3 * SC_WIDTH
GATES_W = 3 * D_MODEL
VT_ROWS = C_QKV + A_KV
W_MAIN_W = QK_A_W + QK_C_W + HBC_W + GATES_W

LANES = 128
PAIR = 2 * HEAD_DIM
BLK = MOBA_BLOCK
HALO = 8

ROW_TILE = 512
VMEM_LIMIT = 56 * 1024 * 1024

NEG_INF = float("-inf")
LOG2E = 1.4426950408889634
Q_SCALE = ATTN_SCALE * LOG2E
MOBA_GROUP = 2
MOBA_PEN_LANE = 8
MOBA_PENALTY = -1e30
SUM_ROWS = 16

SWA_HEAD_ORDER = tuple(kvh * SWA_GROUP + g for g in range(SWA_GROUP) for kvh in range(SWA_KV_HEADS))


def _dot(a, b):
    return jnp.dot(a, b, preferred_element_type=jnp.float32)


def _dot_nt(a, b):
    return lax.dot_general(a, b, (((1,), (1,)), ((), ())), preferred_element_type=jnp.float32)


def _layer_norm(y, g, b):
    mu = jnp.mean(y, axis=-1, keepdims=True)
    d = y - mu
    var = jnp.mean(d * d, axis=-1, keepdims=True)
    return d * lax.rsqrt(var + LN_EPS) * g + b


def _half_mask(shape, half):
    lane = lax.broadcasted_iota(jnp.int32, shape, 1)
    return (lane >= HEAD_DIM) if half else (lane < HEAD_DIM)


def _proj_kernel(x_ref, w_ref, wvt_ref, qka_ref, qkc_ref, hbc_ref, gates_ref, vt_ref):
    xb = x_ref[...].astype(jnp.bfloat16)
    off = 0
    for ref, width in ((qka_ref, QK_A_W), (qkc_ref, QK_C_W), (hbc_ref, HBC_W), (gates_ref, GATES_W)):
        ref[...] = _dot(xb, w_ref[:, off:off + width]).astype(ref.dtype)
        off += width
    for s in range(ROW_TILE // BLK):
        vt_ref[s] = _dot_nt(wvt_ref[...], xb[s * BLK:(s + 1) * BLK]).astype(vt_ref.dtype)


def _project(x2d, w_main, w_vt):
    n = x2d.shape[0]
    const = lambda i: (0, 0)
    row = lambda i: (i, 0)
    bf = jnp.bfloat16
    return pl.pallas_call(
        _proj_kernel,
        grid=(n // ROW_TILE,),
        in_specs=[
            pl.BlockSpec((ROW_TILE, D_MODEL), row),
            pl.BlockSpec((D_MODEL, W_MAIN_W), const, pipeline_mode=pl.Buffered(1)),
            pl.BlockSpec((VT_ROWS, D_MODEL), const, pipeline_mode=pl.Buffered(1)),
        ],
        out_specs=[
            pl.BlockSpec((ROW_TILE, QK_A_W), row),
            pl.BlockSpec((ROW_TILE, QK_C_W), row),
            pl.BlockSpec((ROW_TILE, HBC_W), row),
            pl.BlockSpec((ROW_TILE, GATES_W), row),
            pl.BlockSpec((ROW_TILE // BLK, VT_ROWS, BLK), lambda i: (i, 0, 0)),
        ],
        out_shape=[
            jax.ShapeDtypeStruct((n, QK_A_W), bf),
            jax.ShapeDtypeStruct((n, QK_C_W), bf),
            jax.ShapeDtypeStruct((n, HBC_W), bf),
            jax.ShapeDtypeStruct((n, GATES_W), bf),
            jax.ShapeDtypeStruct((n // BLK, VT_ROWS, BLK), bf),
        ],
        compiler_params=pltpu.CompilerParams(
            dimension_semantics=("arbitrary",), vmem_limit_bytes=VMEM_LIMIT),
        name="in_proj",
    )(x2d, w_main, w_vt)


def _swa_kernel(slopes_ref, sinks_ref, q_ref, kc_ref, kp_ref, vc_ref, vp_ref, o_ref, bias_ref):
    i = pl.program_id(1)
    f32, bf = jnp.float32, jnp.bfloat16

    @pl.when((pl.program_id(0) == 0) & (i == 0))
    def _():
        row = lax.broadcasted_iota(jnp.int32, (SWA_WINDOW + BLK, BLK), 0)
        qi = lax.broadcasted_iota(jnp.int32, (SWA_WINDOW + BLK, BLK), 1)
        dist = qi + SWA_WINDOW - row
        ok = (dist >= 0) & (dist < SWA_WINDOW)
        for h in range(SWA_HEADS):
            bias_ref[h] = jnp.where(ok, -(slopes_ref[h] * LOG2E) * dist.astype(f32), NEG_INF)

    k_cur = kc_ref[...]
    k_prev = kp_ref[BLK - SWA_WINDOW:, :]
    v_cur = vc_ref[0]
    v_prev = vp_ref[0][:, BLK - SWA_WINDOW:]
    ones_c = jnp.ones((SUM_ROWS, BLK), bf)
    ones_p = jnp.ones((SUM_ROWS, SWA_WINDOW), bf)

    def scores(g, kvh):
        h = kvh * SWA_GROUP + g
        q_pair = q_ref[:, g * PAIR:(g + 1) * PAIR]
        qz = jnp.where(_half_mask(q_pair.shape, kvh), q_pair, jnp.zeros_like(q_pair))
        s_c = _dot_nt(k_cur, qz) + bias_ref[h, SWA_WINDOW:]
        s_p = _dot_nt(k_prev, qz) + bias_ref[h, :SWA_WINDOW]
        s_p = jnp.where(i > 0, s_p, NEG_INF)
        m = jnp.maximum(jnp.max(s_c, axis=0, keepdims=True), jnp.max(s_p, axis=0, keepdims=True))
        return s_c, s_p, jnp.maximum(m, sinks_ref[h] * LOG2E)

    def probs(s_c, s_p, m):
        return jnp.exp2(s_c - m).astype(bf), jnp.exp2(s_p - m).astype(bf)

    def values(g, kvh, p_c, p_p, m):
        vh_c = jnp.concatenate([v_cur[kvh * HEAD_DIM:(kvh + 1) * HEAD_DIM], ones_c], axis=0)
        vh_p = jnp.concatenate([v_prev[kvh * HEAD_DIM:(kvh + 1) * HEAD_DIM], ones_p], axis=0)
        acc = _dot(vh_c, p_c) + _dot(vh_p, p_p)
        sink = sinks_ref[kvh * SWA_GROUP + g] * LOG2E
        return acc[:HEAD_DIM] / (acc[HEAD_DIM:HEAD_DIM + 1] + jnp.exp2(sink - m))

    heads = [(g, kvh) for g in range(SWA_GROUP) for kvh in range(SWA_KV_HEADS)]
    s_out, p_out, outs = {}, {}, {}
    for step in range(len(heads) + 2):
        if step < len(heads):
            s_out[step] = scores(*heads[step])
        if 0 <= step - 1 < len(heads):
            s_c, s_p, m = s_out.pop(step - 1)
            p_out[step - 1] = probs(s_c, s_p, m) + (m,)
        if 0 <= step - 2 < len(heads):
            g, kvh = heads[step - 2]
            outs[(g, kvh)] = values(g, kvh, *p_out.pop(step - 2))
            if kvh == SWA_KV_HEADS - 1:
                pair = jnp.concatenate([outs.pop((g, k)) for k in range(SWA_KV_HEADS)], axis=0)
                o_ref[:, g * PAIR:(g + 1) * PAIR] = pair.T.astype(o_ref.dtype)


def _swa(qk_a, vt, slopes, sinks, batch, seq):
    nblk = seq // BLK
    smem = pl.BlockSpec(memory_space=pltpu.SMEM)
    k_col = A_Q // A_KV
    v_row = C_QKV // PAIR
    cur = lambda b, i: (b * nblk + i, 0)
    prev_blk = lambda b, i: b * nblk + jnp.maximum(i - 1, 0)
    return pl.pallas_call(
        _swa_kernel,
        grid=(batch, nblk),
        in_specs=[
            smem, smem,
            pl.BlockSpec((BLK, A_Q), cur),
            pl.BlockSpec((BLK, A_KV), lambda b, i: (b * nblk + i, k_col)),
            pl.BlockSpec((BLK, A_KV), lambda b, i: (prev_blk(b, i), k_col)),
            pl.BlockSpec((1, A_KV, BLK), lambda b, i: (b * nblk + i, v_row, 0)),
            pl.BlockSpec((1, A_KV, BLK), lambda b, i: (prev_blk(b, i), v_row, 0)),
        ],
        out_specs=pl.BlockSpec((BLK, A_Q), cur),
        out_shape=jax.ShapeDtypeStruct((batch * seq, A_Q), jnp.bfloat16),
        scratch_shapes=[pltpu.VMEM((SWA_HEADS, SWA_WINDOW + BLK, BLK), jnp.float32)],
        compiler_params=pltpu.CompilerParams(
            dimension_semantics=("arbitrary", "arbitrary"), vmem_limit_bytes=VMEM_LIMIT),
        name="swa",
    )(slopes, sinks, qk_a, qk_a, qk_a, vt, vt)


def _online_softmax_step(s_ref, shifts, v_ts, m, l, acc):
    rows = [slice(i * BLK, (i + 1) * BLK) for i in range(len(shifts))]
    tile_max = [jnp.max(s_ref[r, :], axis=0, keepdims=True) - shift for r, shift in zip(rows, shifts)]
    m_new = functools.reduce(jnp.maximum, tile_max if m is None else [m] + tile_max)
    l_new, pv = None, None
    for r, shift, v_t in zip(rows, shifts, v_ts):
        p = jnp.exp2(s_ref[r, :] - (m_new + shift))
        psum = jnp.sum(p, axis=0, keepdims=True)
        d = _dot(v_t, p.astype(jnp.bfloat16))
        l_new = psum if l_new is None else l_new + psum
        pv = d if pv is None else pv + d
    if m is None:
        return m_new, l_new, pv
    a = jnp.exp2(m - m_new)
    return m_new, a * l + l_new, a * acc + pv


def _softmax_probs(s_ref, p_ref, tile_max, shifts, m):
    m_new = functools.reduce(jnp.maximum, [m] + [t - shift for t, shift in zip(tile_max, shifts)])
    for i, shift in enumerate(shifts):
        r = slice(i * BLK, (i + 1) * BLK)
        p_ref[r, :] = jnp.exp2(s_ref[r, :] - (m_new + shift)).astype(p_ref.dtype)
    return m_new, jnp.exp2(m - m_new)


def _weighted_values(p_ref, v_ts, a, l, acc):
    ones = jnp.ones((SUM_ROWS, BLK), jnp.bfloat16)
    pv = None
    for i, v_t in enumerate(v_ts):
        d = _dot(jnp.concatenate([v_t, ones], axis=0), p_ref[i * BLK:(i + 1) * BLK, :])
        pv = d if pv is None else pv + d
    return a * l + pv[HEAD_DIM:HEAD_DIM + 1], a * acc + pv[:HEAD_DIM]


def _bf16_pieces(c):
    f32, bf = jnp.float32, jnp.bfloat16
    c1 = c.astype(bf).astype(f32)
    c2 = (c - c1).astype(bf).astype(f32)
    c3 = (c - c1 - c2).astype(bf).astype(f32)
    return c1, c2, c3


def _moba_kernel(slopes_ref, q_ref, k_ref, vt_ref, o_ref,
                 kmean_ref, bias_ref, kaug_ref, qfeat_ref, sown_ref, sa_ref, sb_ref, pa_ref, pb_ref):
    hp = pl.program_id(1)
    n = pl.program_id(2)
    nblk = kmean_ref.shape[0]
    f32, bf = jnp.float32, jnp.bfloat16
    lane_k = lax.broadcasted_iota(jnp.int32, (BLK, PAIR), 1)
    row_k = lax.broadcasted_iota(jnp.int32, (BLK, PAIR), 0).astype(f32)

    def feature_lane(hh):
        return lane_k - (HEAD_DIM if hh == 0 else 0)

    @pl.when(n == 0)
    def _():
        kf = k_ref[...].astype(f32).reshape(nblk, BLK, PAIR)
        kmean_ref[...] = jnp.sum(kf, axis=1) * (1.0 / BLK)
        ki = lax.broadcasted_iota(jnp.int32, (BLK, BLK), 0)
        qi = lax.broadcasted_iota(jnp.int32, (BLK, BLK), 1)
        dist = (qi - ki).astype(f32)
        for hh in range(2):
            c = jnp.full((BLK, PAIR), slopes_ref[2 * hp + hh] * LOG2E, f32)
            bias_ref[hh] = jnp.where(ki <= qi, -c[:, :1] * dist, NEG_INF)
            c1, c2, c3 = _bf16_pieces(c)
            feat = feature_lane(hh)
            pick3 = lambda a0, a1, a2, f0: jnp.where(feat == f0, a0, jnp.where(feat == f0 + 1, a1, a2))
            in_k = (feat >= 0) & (feat < 3)
            in_q = (feat >= 3) & (feat < 6)
            k_feat = jnp.where(in_k, row_k, jnp.where(in_q, -pick3(c1, c2, c3, 3), 0.0))
            qfeat_ref[hh] = jnp.where(in_k, pick3(c1, c2, c3, 0), jnp.where(in_q, row_k, 0.0))
            own_half = _half_mask((BLK, PAIR), hh)

            def fill(j, carry, hh=hh, feat=feat, k_feat=k_feat, own_half=own_half):
                rows = pl.ds(pl.multiple_of(j * BLK, BLK), BLK)
                block_flag = jnp.where(feat == MOBA_PEN_LANE + j, 1.0, k_feat)
                kaug_ref[hh, rows, :] = jnp.where(own_half, k_ref[rows, :], block_flag.astype(bf))
                return carry

            lax.fori_loop(0, nblk, fill, 0)

    q_pair = q_ref[...]
    kmean = kmean_ref[...]
    kmean_hi = kmean.astype(bf)
    kmean_lo = (kmean - kmean_hi.astype(f32)).astype(bf)
    jidx = lax.broadcasted_iota(jnp.int32, (nblk, BLK), 0)
    k_own = k_ref[pl.ds(pl.multiple_of(n * BLK, BLK), BLK), :]
    vt_own = vt_ref[n]

    q_augs = []
    for hh in range(2):
        own_half = _half_mask(q_pair.shape, hh)
        qz = jnp.where(own_half, q_pair, jnp.zeros_like(q_pair))
        gate = _dot_nt(kmean_hi, qz) + _dot_nt(kmean_lo, qz)
        gate = jnp.where(jidx < n, gate, NEG_INF)
        penalty = jnp.full((nblk, BLK), MOBA_PENALTY, f32)
        for _ in range(MOBA_TOPK):
            gmax = jnp.max(gate, axis=0, keepdims=True)
            first = jnp.min(jnp.where(gate == gmax, jidx, nblk), axis=0, keepdims=True)
            pick = (jidx == first) & (gmax > NEG_INF)
            penalty = jnp.where(pick, 0.0, penalty)
            gate = jnp.where(pick, NEG_INF, gate)
        pen0 = (HEAD_DIM if hh == 0 else 0) + MOBA_PEN_LANE
        pen_t = jnp.concatenate([jnp.zeros((pen0, BLK), f32), penalty,
                                 jnp.zeros((PAIR - pen0 - nblk, BLK), f32)], axis=0)
        q_feat = (qfeat_ref[hh] + pen_t.T).astype(bf)
        q_augs.append(jnp.where(own_half, q_pair, q_feat))
        sown_ref[hh] = _dot_nt(k_own, qz) + bias_ref[hh]

    last_group = nblk // MOBA_GROUP - 1

    def group_scores(dst_ref, g):
        j0 = jnp.clip(g, 0, last_group) * MOBA_GROUP
        rows = pl.ds(pl.multiple_of(j0 * BLK, MOBA_GROUP * BLK), MOBA_GROUP * BLK)
        tile_max = []
        for hh in range(2):
            k_g = kaug_ref[hh, rows, :]
            for i in range(MOBA_GROUP):
                s = _dot_nt(k_g[i * BLK:(i + 1) * BLK], q_augs[hh])
                dst_ref[hh, i * BLK:(i + 1) * BLK, :] = s
                tile_max.append(jnp.max(s, axis=0, keepdims=True))
        return tuple(tile_max)

    def group_probs(s_ref, p_ref, g, tile_max, carry):
        j0 = jnp.clip(g, 0, last_group) * MOBA_GROUP
        new = []
        for hh in range(2):
            m, _, l, acc = carry[4 * hh:4 * hh + 4]
            block_shift = slopes_ref[2 * hp + hh] * (LOG2E * BLK)
            shifts = [block_shift * (n - j0 - i).astype(f32) for i in range(MOBA_GROUP)]
            m_new, a = _softmax_probs(s_ref.at[hh], p_ref.at[hh],
                                      tile_max[hh * MOBA_GROUP:(hh + 1) * MOBA_GROUP], shifts, m)
            new += [m_new, a, l, acc]
        return tuple(new)

    def group_values(p_ref, g, carry):
        j0 = jnp.clip(g, 0, last_group) * MOBA_GROUP
        new = []
        for hh in range(2):
            m, a, l, acc = carry[4 * hh:4 * hh + 4]
            v_ts = [vt_ref[j0 + i][hh * HEAD_DIM:(hh + 1) * HEAD_DIM] for i in range(MOBA_GROUP)]
            new += [m, a, *_weighted_values(p_ref.at[hh], v_ts, a, l, acc)]
        return tuple(new)

    n_groups = (n + MOBA_GROUP - 1) // MOBA_GROUP
    first_max = group_scores(sa_ref, 0)
    pb_ref[...] = jnp.zeros(pb_ref.shape, pb_ref.dtype)
    state = []
    for hh in range(2):
        m, l, acc = _online_softmax_step(
            sown_ref.at[hh], [0.0], [vt_own[hh * HEAD_DIM:(hh + 1) * HEAD_DIM]], None, None, None)
        state += [m, jnp.ones_like(m), l, acc]

    def body(t, carry):
        carry, max_a = carry[:8], carry[8:]
        probs = group_probs(sa_ref, pa_ref, 2 * t, max_a, carry)
        pending = group_values(pb_ref, 2 * t - 1, carry)
        carry = tuple(pending[i] if i % 4 >= 2 else probs[i] for i in range(8))
        max_b = group_scores(sb_ref, 2 * t + 1)
        carry = group_values(pa_ref, 2 * t, carry)
        max_a = group_scores(sa_ref, 2 * t + 2)
        return group_probs(sb_ref, pb_ref, 2 * t + 1, max_b, carry) + max_a

    steps = (n_groups + 1) // 2
    state = lax.fori_loop(0, steps, body, tuple(state) + first_max)[:8]
    state = group_values(pb_ref, 2 * steps - 1, state)
    outs = [state[4 * hh + 3] / state[4 * hh + 2] for hh in range(2)]
    o_ref[...] = jnp.concatenate(outs, axis=0).T.astype(o_ref.dtype)


def _moba(qk_c, vt, slopes, batch, seq):
    nblk = seq // BLK
    npair = C_QKV // PAIR
    assert MOBA_PEN_LANE + nblk <= HEAD_DIM and nblk % (2 * MOBA_GROUP) == 0
    smem = pl.BlockSpec(memory_space=pltpu.SMEM)
    return pl.pallas_call(
        _moba_kernel,
        grid=(batch, npair, nblk),
        in_specs=[
            smem,
            pl.BlockSpec((BLK, PAIR), lambda b, hp, n: (b * nblk + n, hp)),
            pl.BlockSpec((seq, PAIR), lambda b, hp, n: (b, npair + hp)),
            pl.BlockSpec((nblk, PAIR, BLK), lambda b, hp, n: (b, hp, 0)),
        ],
        out_specs=pl.BlockSpec((BLK, PAIR), lambda b, hp, n: (b * nblk + n, hp)),
        out_shape=jax.ShapeDtypeStruct((batch * seq, C_QKV), jnp.bfloat16),
        scratch_shapes=[
            pltpu.VMEM((nblk, PAIR), jnp.float32),
            pltpu.VMEM((2, BLK, BLK), jnp.float32),
            pltpu.VMEM((2, seq, PAIR), jnp.bfloat16),
            pltpu.VMEM((2, BLK, PAIR), jnp.float32),
            pltpu.VMEM((2, BLK, BLK), jnp.float32),
            pltpu.VMEM((2, MOBA_GROUP * BLK, BLK), jnp.float32),
            pltpu.VMEM((2, MOBA_GROUP * BLK, BLK), jnp.float32),
            pltpu.VMEM((2, MOBA_GROUP * BLK, BLK), jnp.bfloat16),
            pltpu.VMEM((2, MOBA_GROUP * BLK, BLK), jnp.bfloat16),
        ],
        compiler_params=pltpu.CompilerParams(
            dimension_semantics=("arbitrary", "arbitrary", "arbitrary"), vmem_limit_bytes=VMEM_LIMIT),
        name="moba",
    )(slopes, qk_c, qk_c, vt)


def _merge_kernel(x_ref, ya_ref, yc_ref, hbc_ref, halo_ref, gates_ref, convw_ref,
                  wa_ref, wb_ref, wc_ref, wo_ref, g_ref, b_ref, o_ref, u_ref, *, tiles_per_seq):
    i = pl.program_id(0)
    f32 = jnp.float32

    def gated_input(ref):
        return ref[:, 2 * SC_WIDTH:].astype(f32) * ref[:, :SC_WIDTH].astype(f32)

    hist = jnp.where(i % tiles_per_seq == 0, 0.0, gated_input(halo_ref))
    u_ref[:HALO] = hist
    u_ref[HALO:] = gated_input(hbc_ref)
    conv = (u_ref[HALO - 2:HALO - 2 + ROW_TILE] * convw_ref[0:1, :]
            + u_ref[HALO - 1:HALO - 1 + ROW_TILE] * convw_ref[1:2, :]
            + u_ref[HALO:] * convw_ref[2:3, :])
    y_b = hbc_ref[:, SC_WIDTH:2 * SC_WIDTH].astype(f32) * conv

    def gate(k):
        return jax.nn.sigmoid(gates_ref[:, k * D_MODEL:(k + 1) * D_MODEL].astype(f32))

    merged = (gate(0) * _dot(ya_ref[...], wa_ref[...])
              + gate(1) * _dot(y_b.astype(jnp.bfloat16), wb_ref[...])
              + gate(2) * _dot(yc_ref[...], wc_ref[...]))
    mix = _dot(merged.astype(jnp.bfloat16), wo_ref[...])
    o_ref[...] = _layer_norm(ALPHA * x_ref[...] + mix, g_ref[...], b_ref[...])


def _merge(x2d, y_a, y_c, hbc, gates, conv_w, w_a, w_b, w_c, w_o, ln_g, ln_b, seq):
    n = x2d.shape[0]
    row = lambda i: (i, 0)
    const = lambda i: (0, 0)
    halo_blocks = ROW_TILE // HALO
    weight = lambda shape: pl.BlockSpec(shape, const, pipeline_mode=pl.Buffered(1))
    return pl.pallas_call(
        functools.partial(_merge_kernel, tiles_per_seq=seq // ROW_TILE),
        grid=(n // ROW_TILE,),
        in_specs=[
            pl.BlockSpec((ROW_TILE, D_MODEL), row),
            pl.BlockSpec((ROW_TILE, A_Q), row),
            pl.BlockSpec((ROW_TILE, C_QKV), row),
            pl.BlockSpec((ROW_TILE, HBC_W), row),
            pl.BlockSpec((HALO, HBC_W), lambda i: (jnp.maximum(i * halo_blocks - 1, 0), 0)),
            pl.BlockSpec((ROW_TILE, GATES_W), row),
            weight((3, SC_WIDTH)),
            weight((A_Q, D_MODEL)), weight((SC_WIDTH, D_MODEL)), weight((C_QKV, D_MODEL)),
            weight((D_MODEL, D_MODEL)),
            weight((1, D_MODEL)), weight((1, D_MODEL)),
        ],
        out_specs=pl.BlockSpec((ROW_TILE, D_MODEL), row),
        out_shape=jax.ShapeDtypeStruct((n, D_MODEL), jnp.float32),
        scratch_shapes=[pltpu.VMEM((ROW_TILE + HALO, SC_WIDTH), jnp.float32)],
        compiler_params=pltpu.CompilerParams(
            dimension_semantics=("arbitrary",), vmem_limit_bytes=VMEM_LIMIT),
        name="merge",
    )(x2d, y_a, y_c, hbc, hbc, gates, conv_w, w_a, w_b, w_c, w_o, ln_g, ln_b)


def _ffn_kernel(x_ref, wg_ref, wu_ref, wd_ref, g_ref, b_ref, o_ref):
    x = x_ref[...]
    xb = x.astype(jnp.bfloat16)
    gate = _dot(xb, wg_ref[...])
    up = _dot(xb, wu_ref[...])
    act = (gate * jax.nn.sigmoid(gate) * up).astype(jnp.bfloat16)
    ffn = _dot(act, wd_ref[...])
    o_ref[...] = _layer_norm(ALPHA * x + ffn, g_ref[...], b_ref[...])


def _ffn(x2d, w_gate, w_up, w_down, ln_g, ln_b):
    n = x2d.shape[0]
    row = lambda i: (i, 0)
    const = lambda i: (0, 0)
    weight = lambda shape: pl.BlockSpec(shape, const, pipeline_mode=pl.Buffered(1))
    return pl.pallas_call(
        _ffn_kernel,
        grid=(n // ROW_TILE,),
        in_specs=[
            pl.BlockSpec((ROW_TILE, D_MODEL), row),
            weight((D_MODEL, FFN_HIDDEN)), weight((D_MODEL, FFN_HIDDEN)),
            weight((FFN_HIDDEN, D_MODEL)),
            weight((1, D_MODEL)), weight((1, D_MODEL)),
        ],
        out_specs=pl.BlockSpec((ROW_TILE, D_MODEL), row),
        out_shape=jax.ShapeDtypeStruct((n, D_MODEL), jnp.float32),
        compiler_params=pltpu.CompilerParams(
            dimension_semantics=("arbitrary",), vmem_limit_bytes=VMEM_LIMIT),
        name="ffn",
    )(x2d, w_gate, w_up, w_down, ln_g, ln_b)


def _alibi_slopes():
    i = jnp.arange(N_ALIBI_HEADS, dtype=jnp.float32)
    s = jnp.exp2(-8.0 * (i + 1.0) / N_ALIBI_HEADS)
    return s[:SWA_HEADS], s[SWA_HEADS:]


def _split_w_in(w_in):
    sizes = (A_Q, A_KV, A_KV, SC_WIDTH, SC_WIDTH, SC_WIDTH, C_QKV, C_QKV, C_QKV,
             D_MODEL, D_MODEL, D_MODEL)
    a_q, a_k, a_v, b_h, b_b, b_c, c_q, c_k, c_v, g_a, g_b, g_c = jnp.split(
        w_in, np.cumsum(sizes)[:-1].tolist(), axis=1)
    order = jnp.asarray(SWA_HEAD_ORDER)
    a_q = a_q.reshape(D_MODEL, SWA_HEADS, HEAD_DIM)[:, order].reshape(D_MODEL, A_Q)
    w_main = jnp.concatenate(
        [a_q * Q_SCALE, a_k, c_q * Q_SCALE, c_k, b_h, b_b, b_c, g_a, g_b, g_c], axis=1)
    w_vt = jnp.concatenate([c_v, a_v], axis=1).T
    return w_main.astype(jnp.bfloat16), w_vt.astype(jnp.bfloat16)


def kernel(x, w_in, attn_sinks, conv_w, w_branch_a, w_branch_b, w_branch_c, w_out,
           ln1_g, ln1_b, w_ffn_gate, w_ffn_up, w_ffn_down, ln2_g, ln2_b):
    batch, seq, d = x.shape
    assert d == D_MODEL and seq % ROW_TILE == 0 and seq % BLK == 0
    bf = jnp.bfloat16
    slopes_a, slopes_c = _alibi_slopes()
    order = jnp.asarray(SWA_HEAD_ORDER)
    h = x.reshape(batch * seq, d)
    for l in range(w_in.shape[0]):
        w_main, w_vt = _split_w_in(w_in[l])
        w_a = w_branch_a[l].reshape(SWA_HEADS, HEAD_DIM, d)[order].reshape(A_Q, d).astype(bf)
        qk_a, qk_c, hbc, gates, vt = _project(h, w_main, w_vt)
        y_a = _swa(qk_a, vt, slopes_a, attn_sinks[l].astype(jnp.float32), batch, seq)
        y_c = _moba(qk_c, vt, slopes_c, batch, seq)
        h = _merge(h, y_a, y_c, hbc, gates, conv_w[l], w_a, w_branch_b[l].astype(bf),
                   w_branch_c[l].astype(bf), w_out[l].astype(bf),
                   ln1_g[l].reshape(1, d), ln1_b[l].reshape(1, d), seq)
        h = _ffn(h, w_ffn_gate[l].astype(bf), w_ffn_up[l].astype(bf), w_ffn_down[l].astype(bf),
                 ln2_g[l].reshape(1, d), ln2_b[l].reshape(1, d))
    return h.reshape(batch, seq, d)
```

```python
import functools

import jax
import jax.numpy as jnp
import numpy as np
from jax import lax
from jax.experimental import pallas as pl
from jax.experimental.pallas import tpu as pltpu

D_MODEL = 1024
HEAD_DIM = 64
ATTN_SCALE = HEAD_DIM ** -0.5
SWA_HEADS = 8
SWA_KV_HEADS = 2
SWA_GROUP = SWA_HEADS // SWA_KV_HEADS
SWA_WINDOW = 128
SC_WIDTH = 512
MOBA_HEADS = 8
MOBA_BLOCK = 256
MOBA_TOPK = 3
N_ALIBI_HEADS = SWA_HEADS + MOBA_HEADS
FFN_HIDDEN = 2816
DEPTH = 2
ALPHA = (2 * DEPTH) ** 0.25
LN_EPS = 1e-5

A_Q = SWA_HEADS * HEAD_DIM
A_KV = SWA_KV_HEADS * HEAD_DIM
C_QKV = MOBA_HEADS * HEAD_DIM
QK_A_W = A_Q + A_KV
QK_C_W = 2 * C_QKV
HBC_W = 3 * SC_WIDTH
GATES_W = 3 * D_MODEL
VT_ROWS = C_QKV + A_KV
W_MAIN_W = QK_A_W + QK_C_W + HBC_W + GATES_W

LANES = 128
PAIR = 2 * HEAD_DIM
BLK = MOBA_BLOCK
HALO = 8

ROW_TILE = 512
VMEM_LIMIT = 56 * 1024 * 1024

NEG_INF = float("-inf")
LOG2E = 1.4426950408889634
Q_SCALE = ATTN_SCALE * LOG2E
MOBA_GROUP = 2
MOBA_PEN_LANE = 8
MOBA_PENALTY = -1e30
SUM_ROWS = 16

SWA_HEAD_ORDER = tuple(kvh * SWA_GROUP + g for g in range(SWA_GROUP) for kvh in range(SWA_KV_HEADS))


def _dot(a, b):
    return jnp.dot(a, b, preferred_element_type=jnp.float32)


def _dot_nt(a, b):
    return lax.dot_general(a, b, (((1,), (1,)), ((), ())), preferred_element_type=jnp.float32)


def _layer_norm(y, g, b):
    mu = jnp.mean(y, axis=-1, keepdims=True)
    d = y - mu
    var = jnp.mean(d * d, axis=-1, keepdims=True)
    return d * lax.rsqrt(var + LN_EPS) * g + b


def _half_mask(shape, half):
    lane = lax.broadcasted_iota(jnp.int32, shape, 1)
    return (lane >= HEAD_DIM) if half else (lane < HEAD_DIM)


def _proj_kernel(x_ref, w_ref, wvt_ref, qka_ref, qkc_ref, hbc_ref, gates_ref, vt_ref):
    xb = x_ref[...].astype(jnp.bfloat16)
    off = 0
    for ref, width in ((qka_ref, QK_A_W), (qkc_ref, QK_C_W), (hbc_ref, HBC_W), (gates_ref, GATES_W)):
        ref[...] = _dot(xb, w_ref[:, off:off + width]).astype(ref.dtype)
        off += width
    for s in range(ROW_TILE // BLK):
        vt_ref[s] = _dot_nt(wvt_ref[...], xb[s * BLK:(s + 1) * BLK]).astype(vt_ref.dtype)


def _project(x2d, w_main, w_vt):
    n = x2d.shape[0]
    const = lambda i: (0, 0)
    row = lambda i: (i, 0)
    bf = jnp.bfloat16
    return pl.pallas_call(
        _proj_kernel,
        grid=(n // ROW_TILE,),
        in_specs=[
            pl.BlockSpec((ROW_TILE, D_MODEL), row),
            pl.BlockSpec((D_MODEL, W_MAIN_W), const, pipeline_mode=pl.Buffered(1)),
            pl.BlockSpec((VT_ROWS, D_MODEL), const, pipeline_mode=pl.Buffered(1)),
        ],
        out_specs=[
            pl.BlockSpec((ROW_TILE, QK_A_W), row),
            pl.BlockSpec((ROW_TILE, QK_C_W), row),
            pl.BlockSpec((ROW_TILE, HBC_W), row),
            pl.BlockSpec((ROW_TILE, GATES_W), row),
            pl.BlockSpec((ROW_TILE // BLK, VT_ROWS, BLK), lambda i: (i, 0, 0)),
        ],
        out_shape=[
            jax.ShapeDtypeStruct((n, QK_A_W), bf),
            jax.ShapeDtypeStruct((n, QK_C_W), bf),
            jax.ShapeDtypeStruct((n, HBC_W), bf),
            jax.ShapeDtypeStruct((n, GATES_W), bf),
            jax.ShapeDtypeStruct((n // BLK, VT_ROWS, BLK), bf),
        ],
        compiler_params=pltpu.CompilerParams(
            dimension_semantics=("arbitrary",), vmem_limit_bytes=VMEM_LIMIT),
        name="in_proj",
    )(x2d, w_main, w_vt)


def _swa_kernel(slopes_ref, sinks_ref, q_ref, kc_ref, kp_ref, vc_ref, vp_ref, o_ref, bias_ref):
    i = pl.program_id(1)
    f32, bf = jnp.float32, jnp.bfloat16

    @pl.when((pl.program_id(0) == 0) & (i == 0))
    def _():
        row = lax.broadcasted_iota(jnp.int32, (SWA_WINDOW + BLK, BLK), 0)
        qi = lax.broadcasted_iota(jnp.int32, (SWA_WINDOW + BLK, BLK), 1)
        dist = qi + SWA_WINDOW - row
        ok = (dist >= 0) & (dist < SWA_WINDOW)
        for h in range(SWA_HEADS):
            bias_ref[h] = jnp.where(ok, -(slopes_ref[h] * LOG2E) * dist.astype(f32), NEG_INF)

    k_cur = kc_ref[...]
    k_prev = kp_ref[BLK - SWA_WINDOW:, :]
    v_cur = vc_ref[0]
    v_prev = vp_ref[0][:, BLK - SWA_WINDOW:]
    ones_c = jnp.ones((SUM_ROWS, BLK), bf)
    ones_p = jnp.ones((SUM_ROWS, SWA_WINDOW), bf)

    def scores(g, kvh):
        h = kvh * SWA_GROUP + g
        q_pair = q_ref[:, g * PAIR:(g + 1) * PAIR]
        qz = jnp.where(_half_mask(q_pair.shape, kvh), q_pair, jnp.zeros_like(q_pair))
        s_c = _dot_nt(k_cur, qz) + bias_ref[h, SWA_WINDOW:]
        s_p = _dot_nt(k_prev, qz) + bias_ref[h, :SWA_WINDOW]
        s_p = jnp.where(i > 0, s_p, NEG_INF)
        m = jnp.maximum(jnp.max(s_c, axis=0, keepdims=True), jnp.max(s_p, axis=0, keepdims=True))
        return s_c, s_p, jnp.maximum(m, sinks_ref[h] * LOG2E)

    def probs(s_c, s_p, m):
        return jnp.exp2(s_c - m).astype(bf), jnp.exp2(s_p - m).astype(bf)

    def values(g, kvh, p_c, p_p, m):
        vh_c = jnp.concatenate([v_cur[kvh * HEAD_DIM:(kvh + 1) * HEAD_DIM], ones_c], axis=0)
        vh_p = jnp.concatenate([v_prev[kvh * HEAD_DIM:(kvh + 1) * HEAD_DIM], ones_p], axis=0)
        acc = _dot(vh_c, p_c) + _dot(vh_p, p_p)
        sink = sinks_ref[kvh * SWA_GROUP + g] * LOG2E
        return acc[:HEAD_DIM] / (acc[HEAD_DIM:HEAD_DIM + 1] + jnp.exp2(sink - m))

    heads = [(g, kvh) for g in range(SWA_GROUP) for kvh in range(SWA_KV_HEADS)]
    s_out, p_out, outs = {}, {}, {}
    for step in range(len(heads) + 2):
        if step < len(heads):
            s_out[step] = scores(*heads[step])
        if 0 <= step - 1 < len(heads):
            s_c, s_p, m = s_out.pop(step - 1)
            p_out[step - 1] = probs(s_c, s_p, m) + (m,)
        if 0 <= step - 2 < len(heads):
            g, kvh = heads[step - 2]
            outs[(g, kvh)] = values(g, kvh, *p_out.pop(step - 2))
            if kvh == SWA_KV_HEADS - 1:
                pair = jnp.concatenate([outs.pop((g, k)) for k in range(SWA_KV_HEADS)], axis=0)
                o_ref[:, g * PAIR:(g + 1) * PAIR] = pair.T.astype(o_ref.dtype)


def _swa(qk_a, vt, slopes, sinks, batch, seq):
    nblk = seq // BLK
    smem = pl.BlockSpec(memory_space=pltpu.SMEM)
    k_col = A_Q // A_KV
    v_row = C_QKV // PAIR
    cur = lambda b, i: (b * nblk + i, 0)
    prev_blk = lambda b, i: b * nblk + jnp.maximum(i - 1, 0)
    return pl.pallas_call(
        _swa_kernel,
        grid=(batch, nblk),
        in_specs=[
            smem, smem,
            pl.BlockSpec((BLK, A_Q), cur),
            pl.BlockSpec((BLK, A_KV), lambda b, i: (b * nblk + i, k_col)),
            pl.BlockSpec((BLK, A_KV), lambda b, i: (prev_blk(b, i), k_col)),
            pl.BlockSpec((1, A_KV, BLK), lambda b, i: (b * nblk + i, v_row, 0)),
            pl.BlockSpec((1, A_KV, BLK), lambda b, i: (prev_blk(b, i), v_row, 0)),
        ],
        out_specs=pl.BlockSpec((BLK, A_Q), cur),
        out_shape=jax.ShapeDtypeStruct((batch * seq, A_Q), jnp.bfloat16),
        scratch_shapes=[pltpu.VMEM((SWA_HEADS, SWA_WINDOW + BLK, BLK), jnp.float32)],
        compiler_params=pltpu.CompilerParams(
            dimension_semantics=("arbitrary", "arbitrary"), vmem_limit_bytes=VMEM_LIMIT),
        name="swa",
    )(slopes, sinks, qk_a, qk_a, qk_a, vt, vt)


def _softmax_probs(s_ref, p_ref, tile_max, shifts, m):
    m_new = functools.reduce(jnp.maximum, [m] + [t - shift for t, shift in zip(tile_max, shifts)])
    for i, shift in enumerate(shifts):
        r = slice(i * BLK, (i + 1) * BLK)
        p_ref[r, :] = jnp.exp2(s_ref[r, :] - (m_new + shift)).astype(p_ref.dtype)
    return m_new, jnp.exp2(m - m_new)


def _weighted_values(p_ref, v_ts, a, l, acc):
    ones = jnp.ones((SUM_ROWS, BLK), jnp.bfloat16)
    pv = None
    for i, v_t in enumerate(v_ts):
        d = _dot(jnp.concatenate([v_t, ones], axis=0), p_ref[i * BLK:(i + 1) * BLK, :])
        pv = d if pv is None else pv + d
    return a * l + pv[HEAD_DIM:HEAD_DIM + 1], a * acc + pv[:HEAD_DIM]


def _bf16_pieces(c):
    f32, bf = jnp.float32, jnp.bfloat16
    c1 = c.astype(bf).astype(f32)
    c2 = (c - c1).astype(bf).astype(f32)
    c3 = (c - c1 - c2).astype(bf).astype(f32)
    return c1, c2, c3


def _moba_kernel(slopes_ref, q_ref, qnext_ref, k_ref, vt_ref, o_ref,
                 kmean_ref, bias_ref, kaug_ref, qfeat_ref, qaug_ref, qt_ref,
                 sown_ref, sa_ref, sb_ref, pa_ref, pb_ref):
    hp = pl.program_id(1)
    n = pl.program_id(2)
    nblk = vt_ref.shape[0]
    f32, bf = jnp.float32, jnp.bfloat16
    lane_k = lax.broadcasted_iota(jnp.int32, (BLK, PAIR), 1)
    row_k = lax.broadcasted_iota(jnp.int32, (BLK, PAIR), 0).astype(f32)

    def feature_lane(hh):
        return lane_k - (HEAD_DIM if hh == 0 else 0)

    @pl.when(n == 0)
    def _():
        kf = k_ref[...].astype(f32).reshape(nblk, BLK, PAIR)
        kmean = jnp.sum(kf, axis=1) * (1.0 / BLK)
        kmean_hi = kmean.astype(bf).astype(f32)
        for hh in range(2):
            own = _half_mask(kmean.shape, hh)
            for part, piece in enumerate((kmean_hi, kmean - kmean_hi)):
                rows = slice((2 * hh + part) * nblk, (2 * hh + part + 1) * nblk)
                kmean_ref[rows, :] = jnp.where(own, piece, 0.0).astype(bf)
        ki = lax.broadcasted_iota(jnp.int32, (BLK, BLK), 0)
        qi = lax.broadcasted_iota(jnp.int32, (BLK, BLK), 1)
        dist = (qi - ki).astype(f32)
        for hh in range(2):
            c = jnp.full((BLK, PAIR), slopes_ref[2 * hp + hh] * LOG2E, f32)
            bias_ref[hh] = jnp.where(ki <= qi, -c[:, :1] * dist, NEG_INF)
            c1, c2, c3 = _bf16_pieces(c)
            feat = feature_lane(hh)
            pick3 = lambda a0, a1, a2, f0: jnp.where(feat == f0, a0, jnp.where(feat == f0 + 1, a1, a2))
            in_k = (feat >= 0) & (feat < 3)
            in_q = (feat >= 3) & (feat < 6)
            k_feat = jnp.where(in_k, row_k, jnp.where(in_q, -pick3(c1, c2, c3, 3), 0.0))
            feat_t = lax.broadcasted_iota(jnp.int32, (PAIR, BLK), 0) - (HEAD_DIM if hh == 0 else 0)
            query = lax.broadcasted_iota(jnp.int32, (PAIR, BLK), 1).astype(f32)
            t1, t2, t3 = _bf16_pieces(jnp.full((PAIR, BLK), slopes_ref[2 * hp + hh] * LOG2E, f32))
            slope_piece = jnp.where(feat_t == 0, t1, jnp.where(feat_t == 1, t2, t3))
            qfeat_ref[hh] = jnp.where((feat_t >= 0) & (feat_t < 3), slope_piece,
                                      jnp.where((feat_t >= 3) & (feat_t < 6), query, 0.0))
            own_half = _half_mask((BLK, PAIR), hh)

            def fill(j, carry, hh=hh, feat=feat, k_feat=k_feat, own_half=own_half):
                rows = pl.ds(pl.multiple_of(j * BLK, BLK), BLK)
                block_flag = jnp.where(feat == MOBA_PEN_LANE + j, 1.0, k_feat)
                kaug_ref[hh, rows, :] = jnp.where(own_half, k_ref[rows, :], block_flag.astype(bf))
                return carry

            lax.fori_loop(0, nblk, fill, 0)
        qaug_ref[...] = jnp.zeros(qaug_ref.shape, qaug_ref.dtype)
        qt_ref[...] = q_ref[...].astype(f32).T.astype(bf)

    def own_rows(hh):
        row = lax.broadcasted_iota(jnp.int32, (PAIR, BLK), 0)
        return (row >= HEAD_DIM) if hh else (row < HEAD_DIM)

    def select_blocks(q_pair, n_past):
        jidx = lax.broadcasted_iota(jnp.int32, (nblk, BLK), 0)
        q_t = q_pair.astype(f32).T
        gates = _dot(kmean_ref[...], q_t.astype(bf))
        q_augs = []
        for hh in range(2):
            gate = gates[2 * hh * nblk:(2 * hh + 1) * nblk] + gates[(2 * hh + 1) * nblk:(2 * hh + 2) * nblk]
            gate = jnp.where(jidx < n_past, gate, NEG_INF)
            penalty = jnp.full((nblk, BLK), MOBA_PENALTY, f32)
            for _ in range(MOBA_TOPK):
                gmax = jnp.max(gate, axis=0, keepdims=True)
                first = jnp.min(jnp.where(gate == gmax, jidx, nblk), axis=0, keepdims=True)
                pick = (jidx == first) & (gmax > NEG_INF)
                penalty = jnp.where(pick, 0.0, penalty)
                gate = jnp.where(pick, NEG_INF, gate)
            pen0 = (HEAD_DIM if hh == 0 else 0) + MOBA_PEN_LANE
            pen_t = jnp.concatenate([jnp.zeros((pen0, BLK), f32), penalty,
                                     jnp.zeros((PAIR - pen0 - nblk, BLK), f32)], axis=0)
            q_augs.append(jnp.where(own_rows(hh), q_t, qfeat_ref[hh] + pen_t).astype(bf))
        return q_t.astype(bf), q_augs

    q_augs = [qaug_ref[0], qaug_ref[1]]
    q_t = qt_ref[...]
    last_group = nblk // MOBA_GROUP - 1

    def group_scores(dst_ref, g):
        j0 = jnp.clip(g, 0, last_group) * MOBA_GROUP
        rows = pl.ds(pl.multiple_of(j0 * BLK, MOBA_GROUP * BLK), MOBA_GROUP * BLK)
        tile_max = []
        for hh in range(2):
            k_g = kaug_ref[hh, rows, :]
            for i in range(MOBA_GROUP):
                s = _dot(k_g[i * BLK:(i + 1) * BLK], q_augs[hh])
                dst_ref[hh, i * BLK:(i + 1) * BLK, :] = s
                tile_max.append(jnp.max(s, axis=0, keepdims=True))
        return tuple(tile_max)

    def group_probs(s_ref, p_ref, g, tile_max, carry):
        j0 = jnp.clip(g, 0, last_group) * MOBA_GROUP
        new = []
        for hh in range(2):
            m, _, l, acc = carry[4 * hh:4 * hh + 4]
            block_shift = slopes_ref[2 * hp + hh] * (LOG2E * BLK)
            shifts = [block_shift * (n - j0 - i).astype(f32) for i in range(MOBA_GROUP)]
            m_new, a = _softmax_probs(s_ref.at[hh], p_ref.at[hh],
                                      tile_max[hh * MOBA_GROUP:(hh + 1) * MOBA_GROUP], shifts, m)
            new += [m_new, a, l, acc]
        return tuple(new)

    def group_values(p_ref, g, own, carry):
        j0 = jnp.clip(g, 0, last_group) * MOBA_GROUP
        blocks = [jnp.where(own, n, j0 + i) for i in range(MOBA_GROUP)]
        new = []
        for hh in range(2):
            m, a, l, acc = carry[4 * hh:4 * hh + 4]
            v_ts = [vt_ref[j][hh * HEAD_DIM:(hh + 1) * HEAD_DIM] for j in blocks]
            new += [m, a, *_weighted_values(p_ref.at[hh], v_ts, a, l, acc)]
        return tuple(new)

    n_groups = (n + MOBA_GROUP - 1) // MOBA_GROUP
    first_max = group_scores(sa_ref, 0)
    k_own = k_ref[pl.ds(pl.multiple_of(n * BLK, BLK), BLK), :]
    state = []
    for hh in range(2):
        s_own = _dot(k_own, jnp.where(own_rows(hh), q_t, jnp.zeros_like(q_t))) + bias_ref[hh]
        sown_ref[hh] = s_own
        m, a = _softmax_probs(sown_ref.at[hh], pb_ref.at[hh], [jnp.max(s_own, axis=0, keepdims=True)],
                              [0.0], jnp.full((1, BLK), NEG_INF, f32))
        pb_ref[hh, BLK:, :] = jnp.zeros((pb_ref.shape[1] - BLK, BLK), pb_ref.dtype)
        state += [m, a, jnp.zeros((1, BLK), f32), jnp.zeros((HEAD_DIM, BLK), f32)]

    next_t, next_augs = select_blocks(qnext_ref[...], n + 1)
    qt_ref[...] = next_t
    for hh in range(2):
        qaug_ref[hh] = next_augs[hh]

    def body(t, carry):
        carry, max_a = carry[:8], carry[8:]
        probs = group_probs(sa_ref, pa_ref, 2 * t, max_a, carry)
        pending = group_values(pb_ref, 2 * t - 1, t == 0, carry)
        carry = tuple(pending[i] if i % 4 >= 2 else probs[i] for i in range(8))
        max_b = group_scores(sb_ref, 2 * t + 1)
        carry = group_values(pa_ref, 2 * t, False, carry)
        max_a = group_scores(sa_ref, 2 * t + 2)
        return group_probs(sb_ref, pb_ref, 2 * t + 1, max_b, carry) + max_a

    steps = (n_groups + 1) // 2
    state = lax.fori_loop(0, steps, body, tuple(state) + first_max)[:8]
    state = group_values(pb_ref, 2 * steps - 1, steps == 0, state)
    outs = [state[4 * hh + 3] / state[4 * hh + 2] for hh in range(2)]
    o_ref[...] = jnp.concatenate(outs, axis=0).T.astype(o_ref.dtype)


def _moba(qk_c, vt, slopes, batch, seq):
    nblk = seq // BLK
    npair = C_QKV // PAIR
    assert MOBA_PEN_LANE + nblk <= HEAD_DIM and nblk % (2 * MOBA_GROUP) == 0
    smem = pl.BlockSpec(memory_space=pltpu.SMEM)
    return pl.pallas_call(
        _moba_kernel,
        grid=(batch, npair, nblk),
        in_specs=[
            smem,
            pl.BlockSpec((BLK, PAIR), lambda b, hp, n: (b * nblk + n, hp)),
            pl.BlockSpec((BLK, PAIR), lambda b, hp, n: (b * nblk + jnp.minimum(n + 1, nblk - 1), hp)),
            pl.BlockSpec((seq, PAIR), lambda b, hp, n: (b, npair + hp)),
            pl.BlockSpec((nblk, PAIR, BLK), lambda b, hp, n: (b, hp, 0)),
        ],
        out_specs=pl.BlockSpec((BLK, PAIR), lambda b, hp, n: (b * nblk + n, hp)),
        out_shape=jax.ShapeDtypeStruct((batch * seq, C_QKV), jnp.bfloat16),
        scratch_shapes=[
            pltpu.VMEM((4 * nblk, PAIR), jnp.bfloat16),
            pltpu.VMEM((2, BLK, BLK), jnp.float32),
            pltpu.VMEM((2, seq, PAIR), jnp.bfloat16),
            pltpu.VMEM((2, PAIR, BLK), jnp.float32),
            pltpu.VMEM((2, PAIR, BLK), jnp.bfloat16),
            pltpu.VMEM((PAIR, BLK), jnp.bfloat16),
            pltpu.VMEM((2, BLK, BLK), jnp.float32),
            pltpu.VMEM((2, MOBA_GROUP * BLK, BLK), jnp.float32),
            pltpu.VMEM((2, MOBA_GROUP * BLK, BLK), jnp.float32),
            pltpu.VMEM((2, MOBA_GROUP * BLK, BLK), jnp.bfloat16),
            pltpu.VMEM((2, MOBA_GROUP * BLK, BLK), jnp.bfloat16),
        ],
        compiler_params=pltpu.CompilerParams(
            dimension_semantics=("arbitrary", "arbitrary", "arbitrary"), vmem_limit_bytes=VMEM_LIMIT),
        name="moba",
    )(slopes, qk_c, qk_c, qk_c, vt)


def _merge_kernel(x_ref, ya_ref, yc_ref, hbc_ref, halo_ref, gates_ref, convw_ref,
                  wa_ref, wb_ref, wc_ref, wo_ref, g_ref, b_ref, o_ref, u_ref, *, tiles_per_seq):
    i = pl.program_id(0)
    f32 = jnp.float32

    def gated_input(ref):
        return ref[:, 2 * SC_WIDTH:].astype(f32) * ref[:, :SC_WIDTH].astype(f32)

    hist = jnp.where(i % tiles_per_seq == 0, 0.0, gated_input(halo_ref))
    u_ref[:HALO] = hist
    u_ref[HALO:] = gated_input(hbc_ref)
    conv = (u_ref[HALO - 2:HALO - 2 + ROW_TILE] * convw_ref[0:1, :]
            + u_ref[HALO - 1:HALO - 1 + ROW_TILE] * convw_ref[1:2, :]
            + u_ref[HALO:] * convw_ref[2:3, :])
    y_b = hbc_ref[:, SC_WIDTH:2 * SC_WIDTH].astype(f32) * conv

    def gate(k):
        return jax.nn.sigmoid(gates_ref[:, k * D_MODEL:(k + 1) * D_MODEL].astype(f32))

    merged = (gate(0) * _dot(ya_ref[...], wa_ref[...])
              + gate(1) * _dot(y_b.astype(jnp.bfloat16), wb_ref[...])
              + gate(2) * _dot(yc_ref[...], wc_ref[...]))
    mix = _dot(merged.astype(jnp.bfloat16), wo_ref[...])
    o_ref[...] = _layer_norm(ALPHA * x_ref[...] + mix, g_ref[...], b_ref[...])


def _merge(x2d, y_a, y_c, hbc, gates, conv_w, w_a, w_b, w_c, w_o, ln_g, ln_b, seq):
    n = x2d.shape[0]
    row = lambda i: (i, 0)
    const = lambda i: (0, 0)
    halo_blocks = ROW_TILE // HALO
    weight = lambda shape: pl.BlockSpec(shape, const, pipeline_mode=pl.Buffered(1))
    return pl.pallas_call(
        functools.partial(_merge_kernel, tiles_per_seq=seq // ROW_TILE),
        grid=(n // ROW_TILE,),
        in_specs=[
            pl.BlockSpec((ROW_TILE, D_MODEL), row),
            pl.BlockSpec((ROW_TILE, A_Q), row),
            pl.BlockSpec((ROW_TILE, C_QKV), row),
            pl.BlockSpec((ROW_TILE, HBC_W), row),
            pl.BlockSpec((HALO, HBC_W), lambda i: (jnp.maximum(i * halo_blocks - 1, 0), 0)),
            pl.BlockSpec((ROW_TILE, GATES_W), row),
            weight((3, SC_WIDTH)),
            weight((A_Q, D_MODEL)), weight((SC_WIDTH, D_MODEL)), weight((C_QKV, D_MODEL)),
            weight((D_MODEL, D_MODEL)),
            weight((1, D_MODEL)), weight((1, D_MODEL)),
        ],
        out_specs=pl.BlockSpec((ROW_TILE, D_MODEL), row),
        out_shape=jax.ShapeDtypeStruct((n, D_MODEL), jnp.float32),
        scratch_shapes=[pltpu.VMEM((ROW_TILE + HALO, SC_WIDTH), jnp.float32)],
        compiler_params=pltpu.CompilerParams(
            dimension_semantics=("arbitrary",), vmem_limit_bytes=VMEM_LIMIT),
        name="merge",
    )(x2d, y_a, y_c, hbc, hbc, gates, conv_w, w_a, w_b, w_c, w_o, ln_g, ln_b)


def _ffn_kernel(x_ref, wg_ref, wu_ref, wd_ref, g_ref, b_ref, o_ref):
    x = x_ref[...]
    xb = x.astype(jnp.bfloat16)
    gate = _dot(xb, wg_ref[...])
    up = _dot(xb, wu_ref[...])
    act = (gate * jax.nn.sigmoid(gate) * up).astype(jnp.bfloat16)
    ffn = _dot(act, wd_ref[...])
    o_ref[...] = _layer_norm(ALPHA * x + ffn, g_ref[...], b_ref[...])


def _ffn(x2d, w_gate, w_up, w_down, ln_g, ln_b):
    n = x2d.shape[0]
    row = lambda i: (i, 0)
    const = lambda i: (0, 0)
    weight = lambda shape: pl.BlockSpec(shape, const, pipeline_mode=pl.Buffered(1))
    return pl.pallas_call(
        _ffn_kernel,
        grid=(n // ROW_TILE,),
        in_specs=[
            pl.BlockSpec((ROW_TILE, D_MODEL), row),
            weight((D_MODEL, FFN_HIDDEN)), weight((D_MODEL, FFN_HIDDEN)),
            weight((FFN_HIDDEN, D_MODEL)),
            weight((1, D_MODEL)), weight((1, D_MODEL)),
        ],
        out_specs=pl.BlockSpec((ROW_TILE, D_MODEL), row),
        out_shape=jax.ShapeDtypeStruct((n, D_MODEL), jnp.float32),
        compiler_params=pltpu.CompilerParams(
            dimension_semantics=("arbitrary",), vmem_limit_bytes=VMEM_LIMIT),
        name="ffn",
    )(x2d, w_gate, w_up, w_down, ln_g, ln_b)


def _alibi_slopes():
    i = jnp.arange(N_ALIBI_HEADS, dtype=jnp.float32)
    s = jnp.exp2(-8.0 * (i + 1.0) / N_ALIBI_HEADS)
    return s[:SWA_HEADS], s[SWA_HEADS:]


def _split_w_in(w_in):
    sizes = (A_Q, A_KV, A_KV, SC_WIDTH, SC_WIDTH, SC_WIDTH, C_QKV, C_QKV, C_QKV,
             D_MODEL, D_MODEL, D_MODEL)
    a_q, a_k, a_v, b_h, b_b, b_c, c_q, c_k, c_v, g_a, g_b, g_c = jnp.split(
        w_in, np.cumsum(sizes)[:-1].tolist(), axis=1)
    order = jnp.asarray(SWA_HEAD_ORDER)
    a_q = a_q.reshape(D_MODEL, SWA_HEADS, HEAD_DIM)[:, order].reshape(D_MODEL, A_Q)
    w_main = jnp.concatenate(
        [a_q * Q_SCALE, a_k, c_q * Q_SCALE, c_k, b_h, b_b, b_c, g_a, g_b, g_c], axis=1)
    w_vt = jnp.concatenate([c_v, a_v], axis=1).T
    return w_main.astype(jnp.bfloat16), w_vt.astype(jnp.bfloat16)


def kernel(x, w_in, attn_sinks, conv_w, w_branch_a, w_branch_b, w_branch_c, w_out,
           ln1_g, ln1_b, w_ffn_gate, w_ffn_up, w_ffn_down, ln2_g, ln2_b):
    batch, seq, d = x.shape
    assert d == D_MODEL and seq % ROW_TILE == 0 and seq % BLK == 0
    bf = jnp.bfloat16
    slopes_a, slopes_c = _alibi_slopes()
    order = jnp.asarray(SWA_HEAD_ORDER)
    h = x.reshape(batch * seq, d)
    for l in range(w_in.shape[0]):
        w_main, w_vt = _split_w_in(w_in[l])
        w_a = w_branch_a[l].reshape(SWA_HEADS, HEAD_DIM, d)[order].reshape(A_Q, d).astype(bf)
        qk_a, qk_c, hbc, gates, vt = _project(h, w_main, w_vt)
        y_a = _swa(qk_a, vt, slopes_a, attn_sinks[l].astype(jnp.float32), batch, seq)
        y_c = _moba(qk_c, vt, slopes_c, batch, seq)
        h = _merge(h, y_a, y_c, hbc, gates, conv_w[l], w_a, w_branch_b[l].astype(bf),
                   w_branch_c[l].astype(bf), w_out[l].astype(bf),
                   ln1_g[l].reshape(1, d), ln1_b[l].reshape(1, d), seq)
        h = _ffn(h, w_ffn_gate[l].astype(bf), w_ffn_up[l].astype(bf), w_ffn_down[l].astype(bf),
                 ln2_g[l].reshape(1, d), ln2_b[l].reshape(1, d))
    return h.reshape(batch, seq, d)
```

```python
import functools

import jax
import jax.numpy as jnp
import numpy as np
from jax import lax
from jax.experimental import pallas as pl
from jax.experimental.pallas import tpu as pltpu

D_MODEL = 1024
HEAD_DIM = 64
ATTN_SCALE = HEAD_DIM ** -0.5
SWA_HEADS = 8
SWA_KV_HEADS = 2
SWA_GROUP = SWA_HEADS // SWA_KV_HEADS
SWA_WINDOW = 128
SC_WIDTH = 512
MOBA_HEADS = 8
MOBA_BLOCK = 256
MOBA_TOPK = 3
N_ALIBI_HEADS = SWA_HEADS + MOBA_HEADS
FFN_HIDDEN = 2816
DEPTH = 2
ALPHA = (2 * DEPTH) ** 0.25
LN_EPS = 1e-5

A_Q = SWA_HEADS * HEAD_DIM
A_KV = SWA_KV_HEADS * HEAD_DIM
C_QKV = MOBA_HEADS * HEAD_DIM
QK_A_W = A_Q + A_KV
QK_C_W = 2 * C_QKV
HBC_W = 3 * SC_WIDTH
GATES_W = 3 * D_MODEL
VT_ROWS = C_QKV + A_KV
W_MAIN_W = QK_A_W + QK_C_W + HBC_W + GATES_W

LANES = 128
PAIR = 2 * HEAD_DIM
BLK = MOBA_BLOCK
HALO = 8

ROW_TILE = 512
VMEM_LIMIT = 56 * 1024 * 1024

NEG_INF = float("-inf")
LOG2E = 1.4426950408889634
Q_SCALE = ATTN_SCALE * LOG2E
MOBA_GROUP = 2
MOBA_PEN_LANE = 8
MOBA_PENALTY = -1e30
SUM_ROWS = 16

SWA_HEAD_ORDER = tuple(kvh * SWA_GROUP + g for g in range(SWA_GROUP) for kvh in range(SWA_KV_HEADS))


def _dot(a, b):
    return jnp.dot(a, b, preferred_element_type=jnp.float32)


def _dot_nt(a, b):
    return lax.dot_general(a, b, (((1,), (1,)), ((), ())), preferred_element_type=jnp.float32)


def _layer_norm(y, g, b):
    mu = jnp.mean(y, axis=-1, keepdims=True)
    d = y - mu
    var = jnp.mean(d * d, axis=-1, keepdims=True)
    return d * lax.rsqrt(var + LN_EPS) * g + b


def _half_mask(shape, half):
    lane = lax.broadcasted_iota(jnp.int32, shape, 1)
    return (lane >= HEAD_DIM) if half else (lane < HEAD_DIM)


def _proj_kernel(x_ref, w_ref, wvt_ref, qka_ref, qkc_ref, hbc_ref, gates_ref, vt_ref):
    xb = x_ref[...].astype(jnp.bfloat16)
    off = 0
    for ref, width in ((qka_ref, QK_A_W), (qkc_ref, QK_C_W), (hbc_ref, HBC_W), (gates_ref, GATES_W)):
        ref[...] = _dot(xb, w_ref[:, off:off + width]).astype(ref.dtype)
        off += width
    for s in range(ROW_TILE // BLK):
        vt_ref[s] = _dot_nt(wvt_ref[...], xb[s * BLK:(s + 1) * BLK]).astype(vt_ref.dtype)


def _project(x2d, w_main, w_vt):
    n = x2d.shape[0]
    const = lambda i: (0, 0)
    row = lambda i: (i, 0)
    bf = jnp.bfloat16
    return pl.pallas_call(
        _proj_kernel,
        grid=(n // ROW_TILE,),
        in_specs=[
            pl.BlockSpec((ROW_TILE, D_MODEL), row),
            pl.BlockSpec((D_MODEL, W_MAIN_W), const, pipeline_mode=pl.Buffered(1)),
            pl.BlockSpec((VT_ROWS, D_MODEL), const, pipeline_mode=pl.Buffered(1)),
        ],
        out_specs=[
            pl.BlockSpec((ROW_TILE, QK_A_W), row),
            pl.BlockSpec((ROW_TILE, QK_C_W), row),
            pl.BlockSpec((ROW_TILE, HBC_W), row),
            pl.BlockSpec((ROW_TILE, GATES_W), row),
            pl.BlockSpec((ROW_TILE // BLK, VT_ROWS, BLK), lambda i: (i, 0, 0)),
        ],
        out_shape=[
            jax.ShapeDtypeStruct((n, QK_A_W), bf),
            jax.ShapeDtypeStruct((n, QK_C_W), bf),
            jax.ShapeDtypeStruct((n, HBC_W), bf),
            jax.ShapeDtypeStruct((n, GATES_W), bf),
            jax.ShapeDtypeStruct((n // BLK, VT_ROWS, BLK), bf),
        ],
        compiler_params=pltpu.CompilerParams(
            dimension_semantics=("arbitrary",), vmem_limit_bytes=VMEM_LIMIT),
        name="in_proj",
    )(x2d, w_main, w_vt)


def _swa_kernel(slopes_ref, sinks_ref, q_ref, kc_ref, kp_ref, vc_ref, vp_ref, o_ref, bias_ref):
    i = pl.program_id(1)
    f32, bf = jnp.float32, jnp.bfloat16

    @pl.when((pl.program_id(0) == 0) & (i == 0))
    def _():
        row = lax.broadcasted_iota(jnp.int32, (SWA_WINDOW + BLK, BLK), 0)
        qi = lax.broadcasted_iota(jnp.int32, (SWA_WINDOW + BLK, BLK), 1)
        dist = qi + SWA_WINDOW - row
        ok = (dist >= 0) & (dist < SWA_WINDOW)
        for h in range(SWA_HEADS):
            bias_ref[h] = jnp.where(ok, -(slopes_ref[h] * LOG2E) * dist.astype(f32), NEG_INF)

    k_cur = kc_ref[...]
    k_prev = kp_ref[BLK - SWA_WINDOW:, :]
    v_cur = vc_ref[0]
    v_prev = vp_ref[0][:, BLK - SWA_WINDOW:]
    ones_c = jnp.ones((SUM_ROWS, BLK), bf)
    ones_p = jnp.ones((SUM_ROWS, SWA_WINDOW), bf)

    def scores(g, kvh):
        h = kvh * SWA_GROUP + g
        q_pair = q_ref[:, g * PAIR:(g + 1) * PAIR]
        qz = jnp.where(_half_mask(q_pair.shape, kvh), q_pair, jnp.zeros_like(q_pair))
        s_c = _dot_nt(k_cur, qz) + bias_ref[h, SWA_WINDOW:]
        s_p = _dot_nt(k_prev, qz) + bias_ref[h, :SWA_WINDOW]
        s_p = jnp.where(i > 0, s_p, NEG_INF)
        m = jnp.maximum(jnp.max(s_c, axis=0, keepdims=True), jnp.max(s_p, axis=0, keepdims=True))
        return s_c, s_p, jnp.maximum(m, sinks_ref[h] * LOG2E)

    def probs(s_c, s_p, m):
        return jnp.exp2(s_c - m).astype(bf), jnp.exp2(s_p - m).astype(bf)

    def values(g, kvh, p_c, p_p, m):
        vh_c = jnp.concatenate([v_cur[kvh * HEAD_DIM:(kvh + 1) * HEAD_DIM], ones_c], axis=0)
        vh_p = jnp.concatenate([v_prev[kvh * HEAD_DIM:(kvh + 1) * HEAD_DIM], ones_p], axis=0)
        acc = _dot(vh_c, p_c) + _dot(vh_p, p_p)
        sink = sinks_ref[kvh * SWA_GROUP + g] * LOG2E
        return acc[:HEAD_DIM] / (acc[HEAD_DIM:HEAD_DIM + 1] + jnp.exp2(sink - m))

    heads = [(g, kvh) for g in range(SWA_GROUP) for kvh in range(SWA_KV_HEADS)]
    s_out, p_out, outs = {}, {}, {}
    for step in range(len(heads) + 2):
        if step < len(heads):
            s_out[step] = scores(*heads[step])
        if 0 <= step - 1 < len(heads):
            s_c, s_p, m = s_out.pop(step - 1)
            p_out[step - 1] = probs(s_c, s_p, m) + (m,)
        if 0 <= step - 2 < len(heads):
            g, kvh = heads[step - 2]
            outs[(g, kvh)] = values(g, kvh, *p_out.pop(step - 2))
            if kvh == SWA_KV_HEADS - 1:
                pair = jnp.concatenate([outs.pop((g, k)) for k in range(SWA_KV_HEADS)], axis=0)
                o_ref[:, g * PAIR:(g + 1) * PAIR] = pair.T.astype(o_ref.dtype)


def _swa(qk_a, vt, slopes, sinks, batch, seq):
    nblk = seq // BLK
    smem = pl.BlockSpec(memory_space=pltpu.SMEM)
    k_col = A_Q // A_KV
    v_row = C_QKV // PAIR
    cur = lambda b, i: (b * nblk + i, 0)
    prev_blk = lambda b, i: b * nblk + jnp.maximum(i - 1, 0)
    return pl.pallas_call(
        _swa_kernel,
        grid=(batch, nblk),
        in_specs=[
            smem, smem,
            pl.BlockSpec((BLK, A_Q), cur),
            pl.BlockSpec((BLK, A_KV), lambda b, i: (b * nblk + i, k_col)),
            pl.BlockSpec((BLK, A_KV), lambda b, i: (prev_blk(b, i), k_col)),
            pl.BlockSpec((1, A_KV, BLK), lambda b, i: (b * nblk + i, v_row, 0)),
            pl.BlockSpec((1, A_KV, BLK), lambda b, i: (prev_blk(b, i), v_row, 0)),
        ],
        out_specs=pl.BlockSpec((BLK, A_Q), cur),
        out_shape=jax.ShapeDtypeStruct((batch * seq, A_Q), jnp.bfloat16),
        scratch_shapes=[pltpu.VMEM((SWA_HEADS, SWA_WINDOW + BLK, BLK), jnp.float32)],
        compiler_params=pltpu.CompilerParams(
            dimension_semantics=("arbitrary", "arbitrary"), vmem_limit_bytes=VMEM_LIMIT),
        name="swa",
    )(slopes, sinks, qk_a, qk_a, qk_a, vt, vt)


def _softmax_probs(s_ref, p_ref, tile_max, shifts, m):
    m_new = functools.reduce(jnp.maximum, [m] + [t - shift for t, shift in zip(tile_max, shifts)])
    for i, shift in enumerate(shifts):
        r = slice(i * BLK, (i + 1) * BLK)
        p_ref[r, :] = jnp.exp2(s_ref[r, :] - (m_new + shift)).astype(p_ref.dtype)
    return m_new, jnp.exp2(m - m_new)


def _weighted_values(p_ref, v_ts, a, l, acc):
    ones = jnp.ones((SUM_ROWS, BLK), jnp.bfloat16)
    pv = None
    for i, v_t in enumerate(v_ts):
        d = _dot(jnp.concatenate([v_t, ones], axis=0), p_ref[i * BLK:(i + 1) * BLK, :])
        pv = d if pv is None else pv + d
    return a * l + pv[HEAD_DIM:HEAD_DIM + 1], a * acc + pv[:HEAD_DIM]


def _bf16_pieces(c):
    f32, bf = jnp.float32, jnp.bfloat16
    c1 = c.astype(bf).astype(f32)
    c2 = (c - c1).astype(bf).astype(f32)
    c3 = (c - c1 - c2).astype(bf).astype(f32)
    return c1, c2, c3


def _moba_kernel(slopes_ref, q_ref, qnext_ref, k_ref, vt_ref, o_ref,
                 kmean_ref, bias_ref, kaug_ref, qfeat_ref, qaug_ref, qt_ref,
                 sown_ref, sa_ref, sb_ref, pa_ref, pb_ref):
    hp = pl.program_id(1)
    step = pl.program_id(2)
    nblk = vt_ref.shape[0]
    f32, bf = jnp.float32, jnp.bfloat16
    lane_k = lax.broadcasted_iota(jnp.int32, (BLK, PAIR), 1)
    row_k = lax.broadcasted_iota(jnp.int32, (BLK, PAIR), 0).astype(f32)

    def feature_lane(hh):
        return lane_k - (HEAD_DIM if hh == 0 else 0)

    @pl.when(step == 0)
    def _():
        kf = k_ref[...].astype(f32).reshape(nblk, BLK, PAIR)
        kmean = jnp.sum(kf, axis=1) * (1.0 / BLK)
        kmean_hi = kmean.astype(bf).astype(f32)
        for hh in range(2):
            own = _half_mask(kmean.shape, hh)
            for part, piece in enumerate((kmean_hi, kmean - kmean_hi)):
                rows = slice((2 * hh + part) * nblk, (2 * hh + part + 1) * nblk)
                kmean_ref[rows, :] = jnp.where(own, piece, 0.0).astype(bf)
        ki = lax.broadcasted_iota(jnp.int32, (BLK, BLK), 0)
        qi = lax.broadcasted_iota(jnp.int32, (BLK, BLK), 1)
        dist = (qi - ki).astype(f32)
        for hh in range(2):
            c = jnp.full((BLK, PAIR), slopes_ref[2 * hp + hh] * LOG2E, f32)
            bias_ref[hh] = jnp.where(ki <= qi, -c[:, :1] * dist, NEG_INF)
            c1, c2, c3 = _bf16_pieces(c)
            feat = feature_lane(hh)
            pick3 = lambda a0, a1, a2, f0: jnp.where(feat == f0, a0, jnp.where(feat == f0 + 1, a1, a2))
            in_k = (feat >= 0) & (feat < 3)
            in_q = (feat >= 3) & (feat < 6)
            k_feat = jnp.where(in_k, row_k, jnp.where(in_q, -pick3(c1, c2, c3, 3), 0.0))
            feat_t = lax.broadcasted_iota(jnp.int32, (PAIR, BLK), 0) - (HEAD_DIM if hh == 0 else 0)
            query = lax.broadcasted_iota(jnp.int32, (PAIR, BLK), 1).astype(f32)
            t1, t2, t3 = _bf16_pieces(jnp.full((PAIR, BLK), slopes_ref[2 * hp + hh] * LOG2E, f32))
            slope_piece = jnp.where(feat_t == 0, t1, jnp.where(feat_t == 1, t2, t3))
            qfeat_ref[hh] = jnp.where((feat_t >= 0) & (feat_t < 3), slope_piece,
                                      jnp.where((feat_t >= 3) & (feat_t < 6), query, 0.0))
            own_half = _half_mask((BLK, PAIR), hh)

            def fill(j, carry, hh=hh, feat=feat, k_feat=k_feat, own_half=own_half):
                rows = pl.ds(pl.multiple_of(j * BLK, BLK), BLK)
                block_flag = jnp.where(feat == MOBA_PEN_LANE + j, 1.0, k_feat)
                kaug_ref[hh, rows, :] = jnp.where(own_half, k_ref[rows, :], block_flag.astype(bf))
                return carry

            lax.fori_loop(0, nblk, fill, 0)
        qaug_ref[...] = jnp.zeros(qaug_ref.shape, qaug_ref.dtype)
        qt_ref[...] = q_ref[:BLK, :].astype(f32).T.astype(bf)

    def own_rows(hh):
        row = lax.broadcasted_iota(jnp.int32, (PAIR, BLK), 0)
        return (row >= HEAD_DIM) if hh else (row < HEAD_DIM)

    def select_blocks(q_pair, n_past):
        jidx = lax.broadcasted_iota(jnp.int32, (nblk, BLK), 0)
        q_t = q_pair.astype(f32).T
        gates = _dot(kmean_ref[...], q_t.astype(bf))
        q_augs = []
        for hh in range(2):
            gate = gates[2 * hh * nblk:(2 * hh + 1) * nblk] + gates[(2 * hh + 1) * nblk:(2 * hh + 2) * nblk]
            gate = jnp.where(jidx < n_past, gate, NEG_INF)
            penalty = jnp.full((nblk, BLK), MOBA_PENALTY, f32)
            for _ in range(MOBA_TOPK):
                gmax = jnp.max(gate, axis=0, keepdims=True)
                first = jnp.min(jnp.where(gate == gmax, jidx, nblk), axis=0, keepdims=True)
                pick = (jidx == first) & (gmax > NEG_INF)
                penalty = jnp.where(pick, 0.0, penalty)
                gate = jnp.where(pick, NEG_INF, gate)
            pen0 = (HEAD_DIM if hh == 0 else 0) + MOBA_PEN_LANE
            pen_t = jnp.concatenate([jnp.zeros((pen0, BLK), f32), penalty,
                                     jnp.zeros((PAIR - pen0 - nblk, BLK), f32)], axis=0)
            q_augs.append(jnp.where(own_rows(hh), q_t, qfeat_ref[hh] + pen_t).astype(bf))
        return q_t.astype(bf), q_augs

    def one_block(n, q_next, out_rows):
        q_augs = [qaug_ref[0], qaug_ref[1]]
        q_t = qt_ref[...]
        last_group = nblk // MOBA_GROUP - 1

        def group_scores(dst_ref, g):
            j0 = jnp.clip(g, 0, last_group) * MOBA_GROUP
            rows = pl.ds(pl.multiple_of(j0 * BLK, MOBA_GROUP * BLK), MOBA_GROUP * BLK)
            tile_max = []
            for hh in range(2):
                k_g = kaug_ref[hh, rows, :]
                for i in range(MOBA_GROUP):
                    s = _dot(k_g[i * BLK:(i + 1) * BLK], q_augs[hh])
                    dst_ref[hh, i * BLK:(i + 1) * BLK, :] = s
                    tile_max.append(jnp.max(s, axis=0, keepdims=True))
            return tuple(tile_max)

        def group_probs(s_ref, p_ref, g, tile_max, carry):
            j0 = jnp.clip(g, 0, last_group) * MOBA_GROUP
            new = []
            for hh in range(2):
                m, _, l, acc = carry[4 * hh:4 * hh + 4]
                block_shift = slopes_ref[2 * hp + hh] * (LOG2E * BLK)
                shifts = [block_shift * (n - j0 - i).astype(f32) for i in range(MOBA_GROUP)]
                m_new, a = _softmax_probs(s_ref.at[hh], p_ref.at[hh],
                                          tile_max[hh * MOBA_GROUP:(hh + 1) * MOBA_GROUP], shifts, m)
                new += [m_new, a, l, acc]
            return tuple(new)

        def group_values(p_ref, g, own, carry):
            j0 = jnp.clip(g, 0, last_group) * MOBA_GROUP
            blocks = [jnp.where(own, n, j0 + i) for i in range(MOBA_GROUP)]
            new = []
            for hh in range(2):
                m, a, l, acc = carry[4 * hh:4 * hh + 4]
                v_ts = [vt_ref[j][hh * HEAD_DIM:(hh + 1) * HEAD_DIM] for j in blocks]
                new += [m, a, *_weighted_values(p_ref.at[hh], v_ts, a, l, acc)]
            return tuple(new)

        n_groups = (n + MOBA_GROUP - 1) // MOBA_GROUP
        first_max = group_scores(sa_ref, 0)
        k_own = k_ref[pl.ds(pl.multiple_of(n * BLK, BLK), BLK), :]
        state = []
        for hh in range(2):
            s_own = _dot(k_own, jnp.where(own_rows(hh), q_t, jnp.zeros_like(q_t))) + bias_ref[hh]
            sown_ref[hh] = s_own
            m, a = _softmax_probs(sown_ref.at[hh], pb_ref.at[hh],
                                  [jnp.max(s_own, axis=0, keepdims=True)], [0.0],
                                  jnp.full((1, BLK), NEG_INF, f32))
            pb_ref[hh, BLK:, :] = jnp.zeros((pb_ref.shape[1] - BLK, BLK), pb_ref.dtype)
            state += [m, a, jnp.zeros((1, BLK), f32), jnp.zeros((HEAD_DIM, BLK), f32)]

        next_t, next_augs = select_blocks(q_next, n + 1)
        qt_ref[...] = next_t
        for hh in range(2):
            qaug_ref[hh] = next_augs[hh]

        def body(t, carry):
            carry, max_a = carry[:8], carry[8:]
            probs = group_probs(sa_ref, pa_ref, 2 * t, max_a, carry)
            pending = group_values(pb_ref, 2 * t - 1, t == 0, carry)
            carry = tuple(pending[i] if i % 4 >= 2 else probs[i] for i in range(8))
            max_b = group_scores(sb_ref, 2 * t + 1)
            carry = group_values(pa_ref, 2 * t, False, carry)
            max_a = group_scores(sa_ref, 2 * t + 2)
            return group_probs(sb_ref, pb_ref, 2 * t + 1, max_b, carry) + max_a

        steps = (n_groups + 1) // 2
        state = lax.fori_loop(0, steps, body, tuple(state) + first_max)[:8]
        state = group_values(pb_ref, 2 * steps - 1, steps == 0, state)
        outs = [state[4 * hh + 3] / state[4 * hh + 2] for hh in range(2)]
        o_ref[out_rows, :] = jnp.concatenate(outs, axis=0).T.astype(o_ref.dtype)

    one_block(2 * step, q_ref[BLK:, :], slice(0, BLK))

    @pl.when(step >= 0)
    def _():
        one_block(2 * step + 1, qnext_ref[...], slice(BLK, 2 * BLK))


def _moba(qk_c, vt, slopes, batch, seq):
    nblk = seq // BLK
    npair = C_QKV // PAIR
    assert MOBA_PEN_LANE + nblk <= HEAD_DIM and nblk % (2 * MOBA_GROUP) == 0
    smem = pl.BlockSpec(memory_space=pltpu.SMEM)
    return pl.pallas_call(
        _moba_kernel,
        grid=(batch, npair, nblk // 2),
        in_specs=[
            smem,
            pl.BlockSpec((2 * BLK, PAIR), lambda b, hp, s: (b * (nblk // 2) + s, hp)),
            pl.BlockSpec((BLK, PAIR), lambda b, hp, s: (b * nblk + jnp.minimum(2 * s + 2, nblk - 1), hp)),
            pl.BlockSpec((seq, PAIR), lambda b, hp, s: (b, npair + hp)),
            pl.BlockSpec((nblk, PAIR, BLK), lambda b, hp, s: (b, hp, 0)),
        ],
        out_specs=pl.BlockSpec((2 * BLK, PAIR), lambda b, hp, s: (b * (nblk // 2) + s, hp)),
        out_shape=jax.ShapeDtypeStruct((batch * seq, C_QKV), jnp.bfloat16),
        scratch_shapes=[
            pltpu.VMEM((4 * nblk, PAIR), jnp.bfloat16),
            pltpu.VMEM((2, BLK, BLK), jnp.float32),
            pltpu.VMEM((2, seq, PAIR), jnp.bfloat16),
            pltpu.VMEM((2, PAIR, BLK), jnp.float32),
            pltpu.VMEM((2, PAIR, BLK), jnp.bfloat16),
            pltpu.VMEM((PAIR, BLK), jnp.bfloat16),
            pltpu.VMEM((2, BLK, BLK), jnp.float32),
            pltpu.VMEM((2, MOBA_GROUP * BLK, BLK), jnp.float32),
            pltpu.VMEM((2, MOBA_GROUP * BLK, BLK), jnp.float32),
            pltpu.VMEM((2, MOBA_GROUP * BLK, BLK), jnp.bfloat16),
            pltpu.VMEM((2, MOBA_GROUP * BLK, BLK), jnp.bfloat16),
        ],
        compiler_params=pltpu.CompilerParams(
            dimension_semantics=("arbitrary", "arbitrary", "arbitrary"), vmem_limit_bytes=VMEM_LIMIT),
        name="moba",
    )(slopes, qk_c, qk_c, qk_c, vt)


def _merge_kernel(x_ref, ya_ref, yc_ref, hbc_ref, halo_ref, gates_ref, convw_ref,
                  wa_ref, wb_ref, wc_ref, wo_ref, g_ref, b_ref, o_ref, u_ref, *, tiles_per_seq):
    i = pl.program_id(0)
    f32 = jnp.float32

    def gated_input(ref):
        return ref[:, 2 * SC_WIDTH:].astype(f32) * ref[:, :SC_WIDTH].astype(f32)

    hist = jnp.where(i % tiles_per_seq == 0, 0.0, gated_input(halo_ref))
    u_ref[:HALO] = hist
    u_ref[HALO:] = gated_input(hbc_ref)
    conv = (u_ref[HALO - 2:HALO - 2 + ROW_TILE] * convw_ref[0:1, :]
            + u_ref[HALO - 1:HALO - 1 + ROW_TILE] * convw_ref[1:2, :]
            + u_ref[HALO:] * convw_ref[2:3, :])
    y_b = hbc_ref[:, SC_WIDTH:2 * SC_WIDTH].astype(f32) * conv

    def gate(k):
        return jax.nn.sigmoid(gates_ref[:, k * D_MODEL:(k + 1) * D_MODEL].astype(f32))

    merged = (gate(0) * _dot(ya_ref[...], wa_ref[...])
              + gate(1) * _dot(y_b.astype(jnp.bfloat16), wb_ref[...])
              + gate(2) * _dot(yc_ref[...], wc_ref[...]))
    mix = _dot(merged.astype(jnp.bfloat16), wo_ref[...])
    o_ref[...] = _layer_norm(ALPHA * x_ref[...] + mix, g_ref[...], b_ref[...])


def _merge(x2d, y_a, y_c, hbc, gates, conv_w, w_a, w_b, w_c, w_o, ln_g, ln_b, seq):
    n = x2d.shape[0]
    row = lambda i: (i, 0)
    const = lambda i: (0, 0)
    halo_blocks = ROW_TILE // HALO
    weight = lambda shape: pl.BlockSpec(shape, const, pipeline_mode=pl.Buffered(1))
    return pl.pallas_call(
        functools.partial(_merge_kernel, tiles_per_seq=seq // ROW_TILE),
        grid=(n // ROW_TILE,),
        in_specs=[
            pl.BlockSpec((ROW_TILE, D_MODEL), row),
            pl.BlockSpec((ROW_TILE, A_Q), row),
            pl.BlockSpec((ROW_TILE, C_QKV), row),
            pl.BlockSpec((ROW_TILE, HBC_W), row),
            pl.BlockSpec((HALO, HBC_W), lambda i: (jnp.maximum(i * halo_blocks - 1, 0), 0)),
            pl.BlockSpec((ROW_TILE, GATES_W), row),
            weight((3, SC_WIDTH)),
            weight((A_Q, D_MODEL)), weight((SC_WIDTH, D_MODEL)), weight((C_QKV, D_MODEL)),
            weight((D_MODEL, D_MODEL)),
            weight((1, D_MODEL)), weight((1, D_MODEL)),
        ],
        out_specs=pl.BlockSpec((ROW_TILE, D_MODEL), row),
        out_shape=jax.ShapeDtypeStruct((n, D_MODEL), jnp.float32),
        scratch_shapes=[pltpu.VMEM((ROW_TILE + HALO, SC_WIDTH), jnp.float32)],
        compiler_params=pltpu.CompilerParams(
            dimension_semantics=("arbitrary",), vmem_limit_bytes=VMEM_LIMIT),
        name="merge",
    )(x2d, y_a, y_c, hbc, hbc, gates, conv_w, w_a, w_b, w_c, w_o, ln_g, ln_b)


def _ffn_kernel(x_ref, wg_ref, wu_ref, wd_ref, g_ref, b_ref, o_ref):
    x = x_ref[...]
    xb = x.astype(jnp.bfloat16)
    gate = _dot(xb, wg_ref[...])
    up = _dot(xb, wu_ref[...])
    act = (gate * jax.nn.sigmoid(gate) * up).astype(jnp.bfloat16)
    ffn = _dot(act, wd_ref[...])
    o_ref[...] = _layer_norm(ALPHA * x + ffn, g_ref[...], b_ref[...])


def _ffn(x2d, w_gate, w_up, w_down, ln_g, ln_b):
    n = x2d.shape[0]
    row = lambda i: (i, 0)
    const = lambda i: (0, 0)
    weight = lambda shape: pl.BlockSpec(shape, const, pipeline_mode=pl.Buffered(1))
    return pl.pallas_call(
        _ffn_kernel,
        grid=(n // ROW_TILE,),
        in_specs=[
            pl.BlockSpec((ROW_TILE, D_MODEL), row),
            weight((D_MODEL, FFN_HIDDEN)), weight((D_MODEL, FFN_HIDDEN)),
            weight((FFN_HIDDEN, D_MODEL)),
            weight((1, D_MODEL)), weight((1, D_MODEL)),
        ],
        out_specs=pl.BlockSpec((ROW_TILE, D_MODEL), row),
        out_shape=jax.ShapeDtypeStruct((n, D_MODEL), jnp.float32),
        compiler_params=pltpu.CompilerParams(
            dimension_semantics=("arbitrary",), vmem_limit_bytes=VMEM_LIMIT),
        name="ffn",
    )(x2d, w_gate, w_up, w_down, ln_g, ln_b)


def _alibi_slopes():
    i = jnp.arange(N_ALIBI_HEADS, dtype=jnp.float32)
    s = jnp.exp2(-8.0 * (i + 1.0) / N_ALIBI_HEADS)
    return s[:SWA_HEADS], s[SWA_HEADS:]


def _split_w_in(w_in):
    sizes = (A_Q, A_KV, A_KV, SC_WIDTH, SC_WIDTH, SC_WIDTH, C_QKV, C_QKV, C_QKV,
             D_MODEL, D_MODEL, D_MODEL)
    a_q, a_k, a_v, b_h, b_b, b_c, c_q, c_k, c_v, g_a, g_b, g_c = jnp.split(
        w_in, np.cumsum(sizes)[:-1].tolist(), axis=1)
    order = jnp.asarray(SWA_HEAD_ORDER)
    a_q = a_q.reshape(D_MODEL, SWA_HEADS, HEAD_DIM)[:, order].reshape(D_MODEL, A_Q)
    w_main = jnp.concatenate(
        [a_q * Q_SCALE, a_k, c_q * Q_SCALE, c_k, b_h, b_b, b_c, g_a, g_b, g_c], axis=1)
    w_vt = jnp.concatenate([c_v, a_v], axis=1).T
    return w_main.astype(jnp.bfloat16), w_vt.astype(jnp.bfloat16)


def kernel(x, w_in, attn_sinks, conv_w, w_branch_a, w_branch_b, w_branch_c, w_out,
           ln1_g, ln1_b, w_ffn_gate, w_ffn_up, w_ffn_down, ln2_g, ln2_b):
    batch, seq, d = x.shape
    assert d == D_MODEL and seq % ROW_TILE == 0 and seq % BLK == 0
    bf = jnp.bfloat16
    slopes_a, slopes_c = _alibi_slopes()
    order = jnp.asarray(SWA_HEAD_ORDER)
    h = x.reshape(batch * seq, d)
    for l in range(w_in.shape[0]):
        w_main, w_vt = _split_w_in(w_in[l])
        w_a = w_branch_a[l].reshape(SWA_HEADS, HEAD_DIM, d)[order].reshape(A_Q, d).astype(bf)
        qk_a, qk_c, hbc, gates, vt = _project(h, w_main, w_vt)
        y_a = _swa(qk_a, vt, slopes_a, attn_sinks[l].astype(jnp.float32), batch, seq)
        y_c = _moba(qk_c, vt, slopes_c, batch, seq)
        h = _merge(h, y_a, y_c, hbc, gates, conv_w[l], w_a, w_branch_b[l].astype(bf),
                   w_branch_c[l].astype(bf), w_out[l].astype(bf),
                   ln1_g[l].reshape(1, d), ln1_b[l].reshape(1, d), seq)
        h = _ffn(h, w_ffn_gate[l].astype(bf), w_ffn_up[l].astype(bf), w_ffn_down[l].astype(bf),
                 ln2_g[l].reshape(1, d), ln2_b[l].reshape(1, d))
    return h.reshape(batch, seq, d)
```

```python
import functools

import jax
import jax.numpy as jnp
import numpy as np
from jax import lax
from jax.experimental import pallas as pl
from jax.experimental.pallas import tpu as pltpu

D_MODEL = 1024
HEAD_DIM = 64
ATTN_SCALE = HEAD_DIM ** -0.5
SWA_HEADS = 8
SWA_KV_HEADS = 2
SWA_GROUP = SWA_HEADS // SWA_KV_HEADS
SWA_WINDOW = 128
SC_WIDTH = 512
MOBA_HEADS = 8
MOBA_BLOCK = 256
MOBA_TOPK = 3
N_ALIBI_HEADS = SWA_HEADS + MOBA_HEADS
FFN_HIDDEN = 2816
DEPTH = 2
ALPHA = (2 * DEPTH) ** 0.25
LN_EPS = 1e-5

A_Q = SWA_HEADS * HEAD_DIM
A_KV = SWA_KV_HEADS * HEAD_DIM
C_QKV = MOBA_HEADS * HEAD_DIM
QK_A_W = A_Q + A_KV
QK_C_W = 2 * C_QKV
HBC_W = 3 * SC_WIDTH
GATES_W = 3 * D_MODEL
VT_ROWS = C_QKV + A_KV
W_MAIN_W = QK_A_W + QK_C_W + HBC_W + GATES_W

LANES = 128
PAIR = 2 * HEAD_DIM
BLK = MOBA_BLOCK
HALO = 8

ROW_TILE = 512
VMEM_LIMIT = 56 * 1024 * 1024

NEG_INF = float("-inf")
LOG2E = 1.4426950408889634
Q_SCALE = ATTN_SCALE * LOG2E
MOBA_GROUP = 2
MOBA_PEN_LANE = 8
MOBA_PENALTY = -1e30
SUM_ROWS = 16

SWA_HEAD_ORDER = tuple(kvh * SWA_GROUP + g for g in range(SWA_GROUP) for kvh in range(SWA_KV_HEADS))


def _dot(a, b):
    return jnp.dot(a, b, preferred_element_type=jnp.float32)


def _dot_nt(a, b):
    return lax.dot_general(a, b, (((1,), (1,)), ((), ())), preferred_element_type=jnp.float32)


def _layer_norm(y, g, b):
    mu = jnp.mean(y, axis=-1, keepdims=True)
    d = y - mu
    var = jnp.mean(d * d, axis=-1, keepdims=True)
    return d * lax.rsqrt(var + LN_EPS) * g + b


def _half_mask(shape, half):
    lane = lax.broadcasted_iota(jnp.int32, shape, 1)
    return (lane >= HEAD_DIM) if half else (lane < HEAD_DIM)


def _proj_kernel(x_ref, w_ref, wvt_ref, qka_ref, qkc_ref, hbc_ref, gates_ref, vt_ref):
    xb = x_ref[...].astype(jnp.bfloat16)
    off = 0
    for ref, width in ((qka_ref, QK_A_W), (qkc_ref, QK_C_W), (hbc_ref, HBC_W), (gates_ref, GATES_W)):
        ref[...] = _dot(xb, w_ref[:, off:off + width]).astype(ref.dtype)
        off += width
    for s in range(ROW_TILE // BLK):
        vt_ref[s] = _dot_nt(wvt_ref[...], xb[s * BLK:(s + 1) * BLK]).astype(vt_ref.dtype)


def _project(x2d, w_main, w_vt):
    n = x2d.shape[0]
    const = lambda i: (0, 0)
    row = lambda i: (i, 0)
    bf = jnp.bfloat16
    return pl.pallas_call(
        _proj_kernel,
        grid=(n // ROW_TILE,),
        in_specs=[
            pl.BlockSpec((ROW_TILE, D_MODEL), row),
            pl.BlockSpec((D_MODEL, W_MAIN_W), const, pipeline_mode=pl.Buffered(1)),
            pl.BlockSpec((VT_ROWS, D_MODEL), const, pipeline_mode=pl.Buffered(1)),
        ],
        out_specs=[
            pl.BlockSpec((ROW_TILE, QK_A_W), row),
            pl.BlockSpec((ROW_TILE, QK_C_W), row),
            pl.BlockSpec((ROW_TILE, HBC_W), row),
            pl.BlockSpec((ROW_TILE, GATES_W), row),
            pl.BlockSpec((ROW_TILE // BLK, VT_ROWS, BLK), lambda i: (i, 0, 0)),
        ],
        out_shape=[
            jax.ShapeDtypeStruct((n, QK_A_W), bf),
            jax.ShapeDtypeStruct((n, QK_C_W), bf),
            jax.ShapeDtypeStruct((n, HBC_W), bf),
            jax.ShapeDtypeStruct((n, GATES_W), bf),
            jax.ShapeDtypeStruct((n // BLK, VT_ROWS, BLK), bf),
        ],
        compiler_params=pltpu.CompilerParams(
            dimension_semantics=("arbitrary",), vmem_limit_bytes=VMEM_LIMIT),
        name="in_proj",
    )(x2d, w_main, w_vt)


def _swa_kernel(slopes_ref, sinks_ref, q_ref, kc_ref, kp_ref, vc_ref, vp_ref, o_ref, bias_ref):
    i = pl.program_id(1)
    f32, bf = jnp.float32, jnp.bfloat16

    @pl.when((pl.program_id(0) == 0) & (i == 0))
    def _():
        row = lax.broadcasted_iota(jnp.int32, (SWA_WINDOW + BLK, BLK), 0)
        qi = lax.broadcasted_iota(jnp.int32, (SWA_WINDOW + BLK, BLK), 1)
        dist = qi + SWA_WINDOW - row
        ok = (dist >= 0) & (dist < SWA_WINDOW)
        for h in range(SWA_HEADS):
            bias_ref[h] = jnp.where(ok, -(slopes_ref[h] * LOG2E) * dist.astype(f32), NEG_INF)

    k_cur = kc_ref[...]
    k_prev = kp_ref[BLK - SWA_WINDOW:, :]
    v_cur = vc_ref[0]
    v_prev = vp_ref[0][:, BLK - SWA_WINDOW:]
    ones_c = jnp.ones((SUM_ROWS, BLK), bf)
    ones_p = jnp.ones((SUM_ROWS, SWA_WINDOW), bf)

    def scores(g, kvh):
        h = kvh * SWA_GROUP + g
        q_pair = q_ref[:, g * PAIR:(g + 1) * PAIR]
        qz = jnp.where(_half_mask(q_pair.shape, kvh), q_pair, jnp.zeros_like(q_pair))
        s_c = _dot_nt(k_cur, qz) + bias_ref[h, SWA_WINDOW:]
        s_p = _dot_nt(k_prev, qz) + bias_ref[h, :SWA_WINDOW]
        s_p = jnp.where(i > 0, s_p, NEG_INF)
        m = jnp.maximum(jnp.max(s_c, axis=0, keepdims=True), jnp.max(s_p, axis=0, keepdims=True))
        return s_c, s_p, jnp.maximum(m, sinks_ref[h] * LOG2E)

    def probs(s_c, s_p, m):
        return jnp.exp2(s_c - m).astype(bf), jnp.exp2(s_p - m).astype(bf)

    def values(g, kvh, p_c, p_p, m):
        vh_c = jnp.concatenate([v_cur[kvh * HEAD_DIM:(kvh + 1) * HEAD_DIM], ones_c], axis=0)
        vh_p = jnp.concatenate([v_prev[kvh * HEAD_DIM:(kvh + 1) * HEAD_DIM], ones_p], axis=0)
        acc = _dot(vh_c, p_c) + _dot(vh_p, p_p)
        sink = sinks_ref[kvh * SWA_GROUP + g] * LOG2E
        return acc[:HEAD_DIM] / (acc[HEAD_DIM:HEAD_DIM + 1] + jnp.exp2(sink - m))

    heads = [(g, kvh) for g in range(SWA_GROUP) for kvh in range(SWA_KV_HEADS)]
    s_out, p_out, outs = {}, {}, {}
    for step in range(len(heads) + 2):
        if step < len(heads):
            s_out[step] = scores(*heads[step])
        if 0 <= step - 1 < len(heads):
            s_c, s_p, m = s_out.pop(step - 1)
            p_out[step - 1] = probs(s_c, s_p, m) + (m,)
        if 0 <= step - 2 < len(heads):
            g, kvh = heads[step - 2]
            outs[(g, kvh)] = values(g, kvh, *p_out.pop(step - 2))
            if kvh == SWA_KV_HEADS - 1:
                pair = jnp.concatenate([outs.pop((g, k)) for k in range(SWA_KV_HEADS)], axis=0)
                o_ref[:, g * PAIR:(g + 1) * PAIR] = pair.T.astype(o_ref.dtype)


def _swa(qk_a, vt, slopes, sinks, batch, seq):
    nblk = seq // BLK
    smem = pl.BlockSpec(memory_space=pltpu.SMEM)
    k_col = A_Q // A_KV
    v_row = C_QKV // PAIR
    cur = lambda b, i: (b * nblk + i, 0)
    prev_blk = lambda b, i: b * nblk + jnp.maximum(i - 1, 0)
    return pl.pallas_call(
        _swa_kernel,
        grid=(batch, nblk),
        in_specs=[
            smem, smem,
            pl.BlockSpec((BLK, A_Q), cur),
            pl.BlockSpec((BLK, A_KV), lambda b, i: (b * nblk + i, k_col)),
            pl.BlockSpec((BLK, A_KV), lambda b, i: (prev_blk(b, i), k_col)),
            pl.BlockSpec((1, A_KV, BLK), lambda b, i: (b * nblk + i, v_row, 0)),
            pl.BlockSpec((1, A_KV, BLK), lambda b, i: (prev_blk(b, i), v_row, 0)),
        ],
        out_specs=pl.BlockSpec((BLK, A_Q), cur),
        out_shape=jax.ShapeDtypeStruct((batch * seq, A_Q), jnp.bfloat16),
        scratch_shapes=[pltpu.VMEM((SWA_HEADS, SWA_WINDOW + BLK, BLK), jnp.float32)],
        compiler_params=pltpu.CompilerParams(
            dimension_semantics=("arbitrary", "arbitrary"), vmem_limit_bytes=VMEM_LIMIT),
        name="swa",
    )(slopes, sinks, qk_a, qk_a, qk_a, vt, vt)


def _softmax_probs(s_ref, p_ref, tile_max, shifts, m):
    m_new = functools.reduce(jnp.maximum, [m] + [t - shift for t, shift in zip(tile_max, shifts)])
    for i, shift in enumerate(shifts):
        r = slice(i * BLK, (i + 1) * BLK)
        p_ref[r, :] = jnp.exp2(s_ref[r, :] - (m_new + shift)).astype(p_ref.dtype)
    return m_new, jnp.exp2(m - m_new)


def _weighted_values(p_ref, v_ts, a, l, acc):
    ones = jnp.ones((SUM_ROWS, BLK), jnp.bfloat16)
    pv = None
    for i, v_t in enumerate(v_ts):
        d = _dot(jnp.concatenate([v_t, ones], axis=0), p_ref[i * BLK:(i + 1) * BLK, :])
        pv = d if pv is None else pv + d
    return a * l + pv[HEAD_DIM:HEAD_DIM + 1], a * acc + pv[:HEAD_DIM]


def _bf16_pieces(c):
    f32, bf = jnp.float32, jnp.bfloat16
    c1 = c.astype(bf).astype(f32)
    c2 = (c - c1).astype(bf).astype(f32)
    c3 = (c - c1 - c2).astype(bf).astype(f32)
    return c1, c2, c3


def _moba_kernel(slopes_ref, q_ref, qnext_ref, k_ref, vt_ref, o_ref,
                 kmean_ref, bias_ref, kaug_ref, qfeat_ref, qaug_ref, qt_ref,
                 sown_ref, sa_ref, sb_ref, pa_ref, pb_ref):
    hp = pl.program_id(1)
    n = pl.program_id(2)
    nblk = vt_ref.shape[0]
    f32, bf = jnp.float32, jnp.bfloat16
    lane_k = lax.broadcasted_iota(jnp.int32, (BLK, PAIR), 1)
    row_k = lax.broadcasted_iota(jnp.int32, (BLK, PAIR), 0).astype(f32)

    def feature_lane(hh):
        return lane_k - (HEAD_DIM if hh == 0 else 0)

    @pl.when(n == 0)
    def _():
        kf = k_ref[...].astype(f32).reshape(nblk, BLK, PAIR)
        kmean = jnp.sum(kf, axis=1) * (1.0 / BLK)
        kmean_hi = kmean.astype(bf).astype(f32)
        for hh in range(2):
            own = _half_mask(kmean.shape, hh)
            for part, piece in enumerate((kmean_hi, kmean - kmean_hi)):
                rows = slice((2 * hh + part) * nblk, (2 * hh + part + 1) * nblk)
                kmean_ref[rows, :] = jnp.where(own, piece, 0.0).astype(bf)
        ki = lax.broadcasted_iota(jnp.int32, (BLK, BLK), 0)
        qi = lax.broadcasted_iota(jnp.int32, (BLK, BLK), 1)
        dist = (qi - ki).astype(f32)
        for hh in range(2):
            c = jnp.full((BLK, PAIR), slopes_ref[2 * hp + hh] * LOG2E, f32)
            bias_ref[hh] = jnp.where(ki <= qi, -c[:, :1] * dist, NEG_INF)
            c1, c2, c3 = _bf16_pieces(c)
            feat = feature_lane(hh)
            pick3 = lambda a0, a1, a2, f0: jnp.where(feat == f0, a0, jnp.where(feat == f0 + 1, a1, a2))
            in_k = (feat >= 0) & (feat < 3)
            in_q = (feat >= 3) & (feat < 6)
            k_feat = jnp.where(in_k, row_k, jnp.where(in_q, -pick3(c1, c2, c3, 3), 0.0))
            feat_t = lax.broadcasted_iota(jnp.int32, (PAIR, BLK), 0) - (HEAD_DIM if hh == 0 else 0)
            query = lax.broadcasted_iota(jnp.int32, (PAIR, BLK), 1).astype(f32)
            t1, t2, t3 = _bf16_pieces(jnp.full((PAIR, BLK), slopes_ref[2 * hp + hh] * LOG2E, f32))
            slope_piece = jnp.where(feat_t == 0, t1, jnp.where(feat_t == 1, t2, t3))
            qfeat_ref[hh] = jnp.where((feat_t >= 0) & (feat_t < 3), slope_piece,
                                      jnp.where((feat_t >= 3) & (feat_t < 6), query, 0.0))
            own_half = _half_mask((BLK, PAIR), hh)

            def fill(j, carry, hh=hh, feat=feat, k_feat=k_feat, own_half=own_half):
                rows = pl.ds(pl.multiple_of(j * BLK, BLK), BLK)
                block_flag = jnp.where(feat == MOBA_PEN_LANE + j, 1.0, k_feat)
                kaug_ref[hh, rows, :] = jnp.where(own_half, k_ref[rows, :], block_flag.astype(bf))
                return carry

            lax.fori_loop(0, nblk, fill, 0)
        qaug_ref[...] = jnp.zeros(qaug_ref.shape, qaug_ref.dtype)
        qt_ref[...] = q_ref[...].astype(f32).T.astype(bf)

    def own_rows(hh):
        row = lax.broadcasted_iota(jnp.int32, (PAIR, BLK), 0)
        return (row >= HEAD_DIM) if hh else (row < HEAD_DIM)

    def select_blocks(q_pair, n_past):
        jidx = lax.broadcasted_iota(jnp.int32, (nblk, BLK), 0)
        q_t = q_pair.astype(f32).T
        gates = _dot(kmean_ref[...], q_t.astype(bf))
        q_augs = []
        for hh in range(2):
            gate = gates[2 * hh * nblk:(2 * hh + 1) * nblk] + gates[(2 * hh + 1) * nblk:(2 * hh + 2) * nblk]
            gate = jnp.where(jidx < n_past, gate, NEG_INF)
            penalty = jnp.full((nblk, BLK), MOBA_PENALTY, f32)
            for _ in range(MOBA_TOPK):
                gmax = jnp.max(gate, axis=0, keepdims=True)
                first = jnp.min(jnp.where(gate == gmax, jidx, nblk), axis=0, keepdims=True)
                pick = (jidx == first) & (gmax > NEG_INF)
                penalty = jnp.where(pick, 0.0, penalty)
                gate = jnp.where(pick, NEG_INF, gate)
            pen0 = (HEAD_DIM if hh == 0 else 0) + MOBA_PEN_LANE
            pen_t = jnp.concatenate([jnp.zeros((pen0, BLK), f32), penalty,
                                     jnp.zeros((PAIR - pen0 - nblk, BLK), f32)], axis=0)
            q_augs.append(jnp.where(own_rows(hh), q_t, qfeat_ref[hh] + pen_t).astype(bf))
        return q_t.astype(bf), q_augs

    q_augs = [qaug_ref[0], qaug_ref[1]]
    q_t = qt_ref[...]
    last_group = nblk // MOBA_GROUP - 1

    def group_scores(dst_ref, g):
        j0 = jnp.clip(g, 0, last_group) * MOBA_GROUP
        rows = pl.ds(pl.multiple_of(j0 * BLK, MOBA_GROUP * BLK), MOBA_GROUP * BLK)
        tile_max = []
        for hh in range(2):
            k_g = kaug_ref[hh, rows, :]
            for i in range(MOBA_GROUP):
                s = _dot(k_g[i * BLK:(i + 1) * BLK], q_augs[hh])
                dst_ref[hh, i * BLK:(i + 1) * BLK, :] = s
                tile_max.append(jnp.max(s, axis=0, keepdims=True))
        return tuple(tile_max)

    def group_probs(s_ref, p_ref, g, tile_max, carry):
        j0 = jnp.clip(g, 0, last_group) * MOBA_GROUP
        new = []
        for hh in range(2):
            m, _, l, acc = carry[4 * hh:4 * hh + 4]
            block_shift = slopes_ref[2 * hp + hh] * (LOG2E * BLK)
            shifts = [block_shift * (n - j0 - i).astype(f32) for i in range(MOBA_GROUP)]
            m_new, a = _softmax_probs(s_ref.at[hh], p_ref.at[hh],
                                      tile_max[hh * MOBA_GROUP:(hh + 1) * MOBA_GROUP], shifts, m)
            new += [m_new, a, l, acc]
        return tuple(new)

    def group_values(p_ref, g, own, carry):
        j0 = jnp.clip(g, 0, last_group) * MOBA_GROUP
        blocks = [jnp.where(own, n, j0 + i) for i in range(MOBA_GROUP)]
        new = []
        for hh in range(2):
            m, a, l, acc = carry[4 * hh:4 * hh + 4]
            v_ts = [vt_ref[j][hh * HEAD_DIM:(hh + 1) * HEAD_DIM] for j in blocks]
            new += [m, a, *_weighted_values(p_ref.at[hh], v_ts, a, l, acc)]
        return tuple(new)

    n_groups = (n + MOBA_GROUP - 1) // MOBA_GROUP
    first_max = group_scores(sa_ref, 0)
    k_own = k_ref[pl.ds(pl.multiple_of(n * BLK, BLK), BLK), :]
    state = []
    for hh in range(2):
        s_own = _dot(k_own, jnp.where(own_rows(hh), q_t, jnp.zeros_like(q_t))) + bias_ref[hh]
        sown_ref[hh] = s_own
        m, a = _softmax_probs(sown_ref.at[hh], pb_ref.at[hh], [jnp.max(s_own, axis=0, keepdims=True)],
                              [0.0], jnp.full((1, BLK), NEG_INF, f32))
        pb_ref[hh, BLK:, :] = jnp.zeros((pb_ref.shape[1] - BLK, BLK), pb_ref.dtype)
        state += [m, a, jnp.zeros((1, BLK), f32), jnp.zeros((HEAD_DIM, BLK), f32)]

    next_t, next_augs = select_blocks(qnext_ref[...], n + 1)
    qt_ref[...] = next_t
    for hh in range(2):
        qaug_ref[hh] = next_augs[hh]

    def body(t, carry):
        carry, max_a = carry[:8], carry[8:]
        probs = group_probs(sa_ref, pa_ref, 2 * t, max_a, carry)
        pending = group_values(pb_ref, 2 * t - 1, t == 0, carry)
        carry = tuple(pending[i] if i % 4 >= 2 else probs[i] for i in range(8))
        max_b = group_scores(sb_ref, 2 * t + 1)
        carry = group_values(pa_ref, 2 * t, False, carry)
        max_a = group_scores(sa_ref, 2 * t + 2)
        return group_probs(sb_ref, pb_ref, 2 * t + 1, max_b, carry) + max_a

    pairs = n_groups // 2
    state = lax.fori_loop(0, pairs, body, tuple(state) + first_max)

    def flush_pending(args):
        return group_values(pb_ref, 2 * pairs - 1, pairs == 0, args[:8])

    def last_group_then_flush(args):
        carry, max_a = args[:8], args[8:]
        probs = group_probs(sa_ref, pa_ref, 2 * pairs, max_a, carry)
        pending = group_values(pb_ref, 2 * pairs - 1, pairs == 0, carry)
        carry = tuple(pending[i] if i % 4 >= 2 else probs[i] for i in range(8))
        return group_values(pa_ref, 2 * pairs, False, carry)

    state = lax.cond(n_groups % 2 == 1, last_group_then_flush, flush_pending, state)
    outs = [state[4 * hh + 3] / state[4 * hh + 2] for hh in range(2)]
    o_ref[...] = jnp.concatenate(outs, axis=0).T.astype(o_ref.dtype)


def _moba(qk_c, vt, slopes, batch, seq):
    nblk = seq // BLK
    npair = C_QKV // PAIR
    assert MOBA_PEN_LANE + nblk <= HEAD_DIM and nblk % (2 * MOBA_GROUP) == 0
    smem = pl.BlockSpec(memory_space=pltpu.SMEM)
    return pl.pallas_call(
        _moba_kernel,
        grid=(batch, npair, nblk),
        in_specs=[
            smem,
            pl.BlockSpec((BLK, PAIR), lambda b, hp, n: (b * nblk + n, hp)),
            pl.BlockSpec((BLK, PAIR), lambda b, hp, n: (b * nblk + jnp.minimum(n + 1, nblk - 1), hp)),
            pl.BlockSpec((seq, PAIR), lambda b, hp, n: (b, npair + hp)),
            pl.BlockSpec((nblk, PAIR, BLK), lambda b, hp, n: (b, hp, 0)),
        ],
        out_specs=pl.BlockSpec((BLK, PAIR), lambda b, hp, n: (b * nblk + n, hp)),
        out_shape=jax.ShapeDtypeStruct((batch * seq, C_QKV), jnp.bfloat16),
        scratch_shapes=[
            pltpu.VMEM((4 * nblk, PAIR), jnp.bfloat16),
            pltpu.VMEM((2, BLK, BLK), jnp.float32),
            pltpu.VMEM((2, seq, PAIR), jnp.bfloat16),
            pltpu.VMEM((2, PAIR, BLK), jnp.float32),
            pltpu.VMEM((2, PAIR, BLK), jnp.bfloat16),
            pltpu.VMEM((PAIR, BLK), jnp.bfloat16),
            pltpu.VMEM((2, BLK, BLK), jnp.float32),
            pltpu.VMEM((2, MOBA_GROUP * BLK, BLK), jnp.float32),
            pltpu.VMEM((2, MOBA_GROUP * BLK, BLK), jnp.float32),
            pltpu.VMEM((2, MOBA_GROUP * BLK, BLK), jnp.bfloat16),
            pltpu.VMEM((2, MOBA_GROUP * BLK, BLK), jnp.bfloat16),
        ],
        compiler_params=pltpu.CompilerParams(
            dimension_semantics=("arbitrary", "arbitrary", "arbitrary"), vmem_limit_bytes=VMEM_LIMIT),
        name="moba",
    )(slopes, qk_c, qk_c, qk_c, vt)


def _merge_kernel(x_ref, ya_ref, yc_ref, hbc_ref, halo_ref, gates_ref, convw_ref,
                  wa_ref, wb_ref, wc_ref, wo_ref, g_ref, b_ref, o_ref, u_ref, *, tiles_per_seq):
    i = pl.program_id(0)
    f32 = jnp.float32

    def gated_input(ref):
        return ref[:, 2 * SC_WIDTH:].astype(f32) * ref[:, :SC_WIDTH].astype(f32)

    hist = jnp.where(i % tiles_per_seq == 0, 0.0, gated_input(halo_ref))
    u_ref[:HALO] = hist
    u_ref[HALO:] = gated_input(hbc_ref)
    conv = (u_ref[HALO - 2:HALO - 2 + ROW_TILE] * convw_ref[0:1, :]
            + u_ref[HALO - 1:HALO - 1 + ROW_TILE] * convw_ref[1:2, :]
            + u_ref[HALO:] * convw_ref[2:3, :])
    y_b = hbc_ref[:, SC_WIDTH:2 * SC_WIDTH].astype(f32) * conv

    def gate(k):
        return jax.nn.sigmoid(gates_ref[:, k * D_MODEL:(k + 1) * D_MODEL].astype(f32))

    merged = (gate(0) * _dot(ya_ref[...], wa_ref[...])
              + gate(1) * _dot(y_b.astype(jnp.bfloat16), wb_ref[...])
              + gate(2) * _dot(yc_ref[...], wc_ref[...]))
    mix = _dot(merged.astype(jnp.bfloat16), wo_ref[...])
    o_ref[...] = _layer_norm(ALPHA * x_ref[...] + mix, g_ref[...], b_ref[...])


def _merge(x2d, y_a, y_c, hbc, gates, conv_w, w_a, w_b, w_c, w_o, ln_g, ln_b, seq):
    n = x2d.shape[0]
    row = lambda i: (i, 0)
    const = lambda i: (0, 0)
    halo_blocks = ROW_TILE // HALO
    weight = lambda shape: pl.BlockSpec(shape, const, pipeline_mode=pl.Buffered(1))
    return pl.pallas_call(
        functools.partial(_merge_kernel, tiles_per_seq=seq // ROW_TILE),
        grid=(n // ROW_TILE,),
        in_specs=[
            pl.BlockSpec((ROW_TILE, D_MODEL), row),
            pl.BlockSpec((ROW_TILE, A_Q), row),
            pl.BlockSpec((ROW_TILE, C_QKV), row),
            pl.BlockSpec((ROW_TILE, HBC_W), row),
            pl.BlockSpec((HALO, HBC_W), lambda i: (jnp.maximum(i * halo_blocks - 1, 0), 0)),
            pl.BlockSpec((ROW_TILE, GATES_W), row),
            weight((3, SC_WIDTH)),
            weight((A_Q, D_MODEL)), weight((SC_WIDTH, D_MODEL)), weight((C_QKV, D_MODEL)),
            weight((D_MODEL, D_MODEL)),
            weight((1, D_MODEL)), weight((1, D_MODEL)),
        ],
        out_specs=pl.BlockSpec((ROW_TILE, D_MODEL), row),
        out_shape=jax.ShapeDtypeStruct((n, D_MODEL), jnp.float32),
        scratch_shapes=[pltpu.VMEM((ROW_TILE + HALO, SC_WIDTH), jnp.float32)],
        compiler_params=pltpu.CompilerParams(
            dimension_semantics=("arbitrary",), vmem_limit_bytes=VMEM_LIMIT),
        name="merge",
    )(x2d, y_a, y_c, hbc, hbc, gates, conv_w, w_a, w_b, w_c, w_o, ln_g, ln_b)


def _ffn_kernel(x_ref, wg_ref, wu_ref, wd_ref, g_ref, b_ref, o_ref):
    x = x_ref[...]
    xb = x.astype(jnp.bfloat16)
    gate = _dot(xb, wg_ref[...])
    up = _dot(xb, wu_ref[...])
    act = (gate * jax.nn.sigmoid(gate) * up).astype(jnp.bfloat16)
    ffn = _dot(act, wd_ref[...])
    o_ref[...] = _layer_norm(ALPHA * x + ffn, g_ref[...], b_ref[...])


def _ffn(x2d, w_gate, w_up, w_down, ln_g, ln_b):
    n = x2d.shape[0]
    row = lambda i: (i, 0)
    const = lambda i: (0, 0)
    weight = lambda shape: pl.BlockSpec(shape, const, pipeline_mode=pl.Buffered(1))
    return pl.pallas_call(
        _ffn_kernel,
        grid=(n // ROW_TILE,),
        in_specs=[
            pl.BlockSpec((ROW_TILE, D_MODEL), row),
            weight((D_MODEL, FFN_HIDDEN)), weight((D_MODEL, FFN_HIDDEN)),
            weight((FFN_HIDDEN, D_MODEL)),
            weight((1, D_MODEL)), weight((1, D_MODEL)),
        ],
        out_specs=pl.BlockSpec((ROW_TILE, D_MODEL), row),
        out_shape=jax.ShapeDtypeStruct((n, D_MODEL), jnp.float32),
        compiler_params=pltpu.CompilerParams(
            dimension_semantics=("arbitrary",), vmem_limit_bytes=VMEM_LIMIT),
        name="ffn",
    )(x2d, w_gate, w_up, w_down, ln_g, ln_b)


def _alibi_slopes():
    i = jnp.arange(N_ALIBI_HEADS, dtype=jnp.float32)
    s = jnp.exp2(-8.0 * (i + 1.0) / N_ALIBI_HEADS)
    return s[:SWA_HEADS], s[SWA_HEADS:]


def _split_w_in(w_in):
    sizes = (A_Q, A_KV, A_KV, SC_WIDTH, SC_WIDTH, SC_WIDTH, C_QKV, C_QKV, C_QKV,
             D_MODEL, D_MODEL, D_MODEL)
    a_q, a_k, a_v, b_h, b_b, b_c, c_q, c_k, c_v, g_a, g_b, g_c = jnp.split(
        w_in, np.cumsum(sizes)[:-1].tolist(), axis=1)
    order = jnp.asarray(SWA_HEAD_ORDER)
    a_q = a_q.reshape(D_MODEL, SWA_HEADS, HEAD_DIM)[:, order].reshape(D_MODEL, A_Q)
    w_main = jnp.concatenate(
        [a_q * Q_SCALE, a_k, c_q * Q_SCALE, c_k, b_h, b_b, b_c, g_a, g_b, g_c], axis=1)
    w_vt = jnp.concatenate([c_v, a_v], axis=1).T
    return w_main.astype(jnp.bfloat16), w_vt.astype(jnp.bfloat16)


def kernel(x, w_in, attn_sinks, conv_w, w_branch_a, w_branch_b, w_branch_c, w_out,
           ln1_g, ln1_b, w_ffn_gate, w_ffn_up, w_ffn_down, ln2_g, ln2_b):
    batch, seq, d = x.shape
    assert d == D_MODEL and seq % ROW_TILE == 0 and seq % BLK == 0
    bf = jnp.bfloat16
    slopes_a, slopes_c = _alibi_slopes()
    order = jnp.asarray(SWA_HEAD_ORDER)
    h = x.reshape(batch * seq, d)
    for l in range(w_in.shape[0]):
        w_main, w_vt = _split_w_in(w_in[l])
        w_a = w_branch_a[l].reshape(SWA_HEADS, HEAD_DIM, d)[order].reshape(A_Q, d).astype(bf)
        qk_a, qk_c, hbc, gates, vt = _project(h, w_main, w_vt)
        y_a = _swa(qk_a, vt, slopes_a, attn_sinks[l].astype(jnp.float32), batch, seq)
        y_c = _moba(qk_c, vt, slopes_c, batch, seq)
        h = _merge(h, y_a, y_c, hbc, gates, conv_w[l], w_a, w_branch_b[l].astype(bf),
                   w_branch_c[l].astype(bf), w_out[l].astype(bf),
                   ln1_g[l].reshape(1, d), ln1_b[l].reshape(1, d), seq)
        h = _ffn(h, w_ffn_gate[l].astype(bf), w_ffn_up[l].astype(bf), w_ffn_down[l].astype(bf),
                 ln2_g[l].reshape(1, d), ln2_b[l].reshape(1, d))
    return h.reshape(batch, seq, d)
```

```python
import functools

import jax
import jax.numpy as jnp
import numpy as np
from jax import lax
from jax.experimental import pallas as pl
from jax.experimental.pallas import tpu as pltpu

D_MODEL = 1024
HEAD_DIM = 64
ATTN_SCALE = HEAD_DIM ** -0.5
SWA_HEADS = 8
SWA_KV_HEADS = 2
SWA_GROUP = SWA_HEADS // SWA_KV_HEADS
SWA_WINDOW = 128
SC_WIDTH = 512
MOBA_HEADS = 8
MOBA_BLOCK = 256
MOBA_TOPK = 3
N_ALIBI_HEADS = SWA_HEADS + MOBA_HEADS
FFN_HIDDEN = 2816
DEPTH = 2
ALPHA = (2 * DEPTH) ** 0.25
LN_EPS = 1e-5

A_Q = SWA_HEADS * HEAD_DIM
A_KV = SWA_KV_HEADS * HEAD_DIM
C_QKV = MOBA_HEADS * HEAD_DIM
QK_A_W = A_Q + A_KV
QK_C_W = 2 * C_QKV
HBC_W = 3 * SC_WIDTH
GATES_W = 3 * D_MODEL
VT_ROWS = C_QKV + A_KV
W_MAIN_W = QK_A_W + QK_C_W + HBC_W

LANES = 128
PAIR = 2 * HEAD_DIM
BLK = MOBA_BLOCK
HALO = 8

ROW_TILE = 512
VMEM_LIMIT = 56 * 1024 * 1024

NEG_INF = float("-inf")
LOG2E = 1.4426950408889634
Q_SCALE = ATTN_SCALE * LOG2E
MOBA_GROUP = 2
MOBA_PEN_LANE = 8
MOBA_PENALTY = -1e30
SUM_ROWS = 16

SWA_HEAD_ORDER = tuple(kvh * SWA_GROUP + g for g in range(SWA_GROUP) for kvh in range(SWA_KV_HEADS))


def _dot(a, b):
    return jnp.dot(a, b, preferred_element_type=jnp.float32)


def _dot_nt(a, b):
    return lax.dot_general(a, b, (((1,), (1,)), ((), ())), preferred_element_type=jnp.float32)


def _layer_norm(y, g, b):
    mu = jnp.mean(y, axis=-1, keepdims=True)
    d = y - mu
    var = jnp.mean(d * d, axis=-1, keepdims=True)
    return d * lax.rsqrt(var + LN_EPS) * g + b


def _half_mask(shape, half):
    lane = lax.broadcasted_iota(jnp.int32, shape, 1)
    return (lane >= HEAD_DIM) if half else (lane < HEAD_DIM)


def _proj_kernel(x_ref, w_ref, wvt_ref, qka_ref, qkc_ref, hbc_ref, vt_ref):
    xb = x_ref[...].astype(jnp.bfloat16)
    off = 0
    for ref, width in ((qka_ref, QK_A_W), (qkc_ref, QK_C_W), (hbc_ref, HBC_W)):
        ref[...] = _dot(xb, w_ref[:, off:off + width]).astype(ref.dtype)
        off += width
    for s in range(ROW_TILE // BLK):
        vt_ref[s] = _dot_nt(wvt_ref[...], xb[s * BLK:(s + 1) * BLK]).astype(vt_ref.dtype)


def _project(x2d, w_main, w_vt):
    n = x2d.shape[0]
    const = lambda i: (0, 0)
    row = lambda i: (i, 0)
    bf = jnp.bfloat16
    return pl.pallas_call(
        _proj_kernel,
        grid=(n // ROW_TILE,),
        in_specs=[
            pl.BlockSpec((ROW_TILE, D_MODEL), row),
            pl.BlockSpec((D_MODEL, W_MAIN_W), const, pipeline_mode=pl.Buffered(1)),
            pl.BlockSpec((VT_ROWS, D_MODEL), const, pipeline_mode=pl.Buffered(1)),
        ],
        out_specs=[
            pl.BlockSpec((ROW_TILE, QK_A_W), row),
            pl.BlockSpec((ROW_TILE, QK_C_W), row),
            pl.BlockSpec((ROW_TILE, HBC_W), row),
            pl.BlockSpec((ROW_TILE // BLK, VT_ROWS, BLK), lambda i: (i, 0, 0)),
        ],
        out_shape=[
            jax.ShapeDtypeStruct((n, QK_A_W), bf),
            jax.ShapeDtypeStruct((n, QK_C_W), bf),
            jax.ShapeDtypeStruct((n, HBC_W), bf),
            jax.ShapeDtypeStruct((n // BLK, VT_ROWS, BLK), bf),
        ],
        compiler_params=pltpu.CompilerParams(
            dimension_semantics=("arbitrary",), vmem_limit_bytes=VMEM_LIMIT),
        name="in_proj",
    )(x2d, w_main, w_vt)


def _swa_kernel(slopes_ref, sinks_ref, q_ref, kc_ref, kp_ref, vc_ref, vp_ref, o_ref, bias_ref):
    i = pl.program_id(1)
    f32, bf = jnp.float32, jnp.bfloat16

    @pl.when((pl.program_id(0) == 0) & (i == 0))
    def _():
        row = lax.broadcasted_iota(jnp.int32, (SWA_WINDOW + BLK, BLK), 0)
        qi = lax.broadcasted_iota(jnp.int32, (SWA_WINDOW + BLK, BLK), 1)
        dist = qi + SWA_WINDOW - row
        ok = (dist >= 0) & (dist < SWA_WINDOW)
        for h in range(SWA_HEADS):
            bias_ref[h] = jnp.where(ok, -(slopes_ref[h] * LOG2E) * dist.astype(f32), NEG_INF)

    k_cur = kc_ref[...]
    k_prev = kp_ref[BLK - SWA_WINDOW:, :]
    v_cur = vc_ref[0]
    v_prev = vp_ref[0][:, BLK - SWA_WINDOW:]
    ones_c = jnp.ones((SUM_ROWS, BLK), bf)
    ones_p = jnp.ones((SUM_ROWS, SWA_WINDOW), bf)

    def scores(g, kvh):
        h = kvh * SWA_GROUP + g
        q_pair = q_ref[:, g * PAIR:(g + 1) * PAIR]
        qz = jnp.where(_half_mask(q_pair.shape, kvh), q_pair, jnp.zeros_like(q_pair))
        s_c = _dot_nt(k_cur, qz) + bias_ref[h, SWA_WINDOW:]
        s_p = _dot_nt(k_prev, qz) + bias_ref[h, :SWA_WINDOW]
        s_p = jnp.where(i > 0, s_p, NEG_INF)
        m = jnp.maximum(jnp.max(s_c, axis=0, keepdims=True), jnp.max(s_p, axis=0, keepdims=True))
        return s_c, s_p, jnp.maximum(m, sinks_ref[h] * LOG2E)

    def probs(s_c, s_p, m):
        return jnp.exp2(s_c - m).astype(bf), jnp.exp2(s_p - m).astype(bf)

    def values(g, kvh, p_c, p_p, m):
        vh_c = jnp.concatenate([v_cur[kvh * HEAD_DIM:(kvh + 1) * HEAD_DIM], ones_c], axis=0)
        vh_p = jnp.concatenate([v_prev[kvh * HEAD_DIM:(kvh + 1) * HEAD_DIM], ones_p], axis=0)
        acc = _dot(vh_c, p_c) + _dot(vh_p, p_p)
        sink = sinks_ref[kvh * SWA_GROUP + g] * LOG2E
        return acc[:HEAD_DIM] / (acc[HEAD_DIM:HEAD_DIM + 1] + jnp.exp2(sink - m))

    heads = [(g, kvh) for g in range(SWA_GROUP) for kvh in range(SWA_KV_HEADS)]
    s_out, p_out, outs = {}, {}, {}
    for step in range(len(heads) + 2):
        if step < len(heads):
            s_out[step] = scores(*heads[step])
        if 0 <= step - 1 < len(heads):
            s_c, s_p, m = s_out.pop(step - 1)
            p_out[step - 1] = probs(s_c, s_p, m) + (m,)
        if 0 <= step - 2 < len(heads):
            g, kvh = heads[step - 2]
            outs[(g, kvh)] = values(g, kvh, *p_out.pop(step - 2))
            if kvh == SWA_KV_HEADS - 1:
                pair = jnp.concatenate([outs.pop((g, k)) for k in range(SWA_KV_HEADS)], axis=0)
                o_ref[:, g * PAIR:(g + 1) * PAIR] = pair.T.astype(o_ref.dtype)


def _swa(qk_a, vt, slopes, sinks, batch, seq):
    nblk = seq // BLK
    smem = pl.BlockSpec(memory_space=pltpu.SMEM)
    k_col = A_Q // A_KV
    v_row = C_QKV // PAIR
    cur = lambda b, i: (b * nblk + i, 0)
    prev_blk = lambda b, i: b * nblk + jnp.maximum(i - 1, 0)
    return pl.pallas_call(
        _swa_kernel,
        grid=(batch, nblk),
        in_specs=[
            smem, smem,
            pl.BlockSpec((BLK, A_Q), cur),
            pl.BlockSpec((BLK, A_KV), lambda b, i: (b * nblk + i, k_col)),
            pl.BlockSpec((BLK, A_KV), lambda b, i: (prev_blk(b, i), k_col)),
            pl.BlockSpec((1, A_KV, BLK), lambda b, i: (b * nblk + i, v_row, 0)),
            pl.BlockSpec((1, A_KV, BLK), lambda b, i: (prev_blk(b, i), v_row, 0)),
        ],
        out_specs=pl.BlockSpec((BLK, A_Q), cur),
        out_shape=jax.ShapeDtypeStruct((batch * seq, A_Q), jnp.bfloat16),
        scratch_shapes=[pltpu.VMEM((SWA_HEADS, SWA_WINDOW + BLK, BLK), jnp.float32)],
        compiler_params=pltpu.CompilerParams(
            dimension_semantics=("arbitrary", "arbitrary"), vmem_limit_bytes=VMEM_LIMIT),
        name="swa",
    )(slopes, sinks, qk_a, qk_a, qk_a, vt, vt)


def _softmax_probs(s_ref, p_ref, tile_max, shifts, m):
    m_new = functools.reduce(jnp.maximum, [m] + [t - shift for t, shift in zip(tile_max, shifts)])
    for i, shift in enumerate(shifts):
        r = slice(i * BLK, (i + 1) * BLK)
        p_ref[r, :] = jnp.exp2(s_ref[r, :] - (m_new + shift)).astype(p_ref.dtype)
    return m_new, jnp.exp2(m - m_new)


def _weighted_values(p_ref, v_ts, a, l, acc):
    ones = jnp.ones((SUM_ROWS, BLK), jnp.bfloat16)
    pv = None
    for i, v_t in enumerate(v_ts):
        d = _dot(jnp.concatenate([v_t, ones], axis=0), p_ref[i * BLK:(i + 1) * BLK, :])
        pv = d if pv is None else pv + d
    return a * l + pv[HEAD_DIM:HEAD_DIM + 1], a * acc + pv[:HEAD_DIM]


def _bf16_pieces(c):
    f32, bf = jnp.float32, jnp.bfloat16
    c1 = c.astype(bf).astype(f32)
    c2 = (c - c1).astype(bf).astype(f32)
    c3 = (c - c1 - c2).astype(bf).astype(f32)
    return c1, c2, c3


def _moba_kernel(slopes_ref, q_ref, qnext_ref, k_ref, vt_ref, o_ref,
                 kmean_ref, bias_ref, kaug_ref, qfeat_ref, qaug_ref, qt_ref,
                 sown_ref, sa_ref, sb_ref, pa_ref, pb_ref):
    hp = pl.program_id(1)
    n = pl.program_id(2)
    nblk = vt_ref.shape[0]
    f32, bf = jnp.float32, jnp.bfloat16
    lane_k = lax.broadcasted_iota(jnp.int32, (BLK, PAIR), 1)
    row_k = lax.broadcasted_iota(jnp.int32, (BLK, PAIR), 0).astype(f32)

    def feature_lane(hh):
        return lane_k - (HEAD_DIM if hh == 0 else 0)

    @pl.when(n == 0)
    def _():
        kf = k_ref[...].astype(f32).reshape(nblk, BLK, PAIR)
        kmean = jnp.sum(kf, axis=1) * (1.0 / BLK)
        kmean_hi = kmean.astype(bf).astype(f32)
        for hh in range(2):
            own = _half_mask(kmean.shape, hh)
            for part, piece in enumerate((kmean_hi, kmean - kmean_hi)):
                rows = slice((2 * hh + part) * nblk, (2 * hh + part + 1) * nblk)
                kmean_ref[rows, :] = jnp.where(own, piece, 0.0).astype(bf)
        ki = lax.broadcasted_iota(jnp.int32, (BLK, BLK), 0)
        qi = lax.broadcasted_iota(jnp.int32, (BLK, BLK), 1)
        dist = (qi - ki).astype(f32)
        for hh in range(2):
            c = jnp.full((BLK, PAIR), slopes_ref[2 * hp + hh] * LOG2E, f32)
            bias_ref[hh] = jnp.where(ki <= qi, -c[:, :1] * dist, NEG_INF)
            c1, c2, c3 = _bf16_pieces(c)
            feat = feature_lane(hh)
            pick3 = lambda a0, a1, a2, f0: jnp.where(feat == f0, a0, jnp.where(feat == f0 + 1, a1, a2))
            in_k = (feat >= 0) & (feat < 3)
            in_q = (feat >= 3) & (feat < 6)
            k_feat = jnp.where(in_k, row_k, jnp.where(in_q, -pick3(c1, c2, c3, 3), 0.0))
            feat_t = lax.broadcasted_iota(jnp.int32, (PAIR, BLK), 0) - (HEAD_DIM if hh == 0 else 0)
            query = lax.broadcasted_iota(jnp.int32, (PAIR, BLK), 1).astype(f32)
            t1, t2, t3 = _bf16_pieces(jnp.full((PAIR, BLK), slopes_ref[2 * hp + hh] * LOG2E, f32))
            slope_piece = jnp.where(feat_t == 0, t1, jnp.where(feat_t == 1, t2, t3))
            qfeat_ref[hh] = jnp.where((feat_t >= 0) & (feat_t < 3), slope_piece,
                                      jnp.where((feat_t >= 3) & (feat_t < 6), query, 0.0))
            own_half = _half_mask((BLK, PAIR), hh)

            def fill(j, carry, hh=hh, feat=feat, k_feat=k_feat, own_half=own_half):
                rows = pl.ds(pl.multiple_of(j * BLK, BLK), BLK)
                block_flag = jnp.where(feat == MOBA_PEN_LANE + j, 1.0, k_feat)
                kaug_ref[hh, rows, :] = jnp.where(own_half, k_ref[rows, :], block_flag.astype(bf))
                return carry

            lax.fori_loop(0, nblk, fill, 0)
        qaug_ref[...] = jnp.zeros(qaug_ref.shape, qaug_ref.dtype)
        qt_ref[...] = q_ref[...].astype(f32).T.astype(bf)

    def own_rows(hh):
        row = lax.broadcasted_iota(jnp.int32, (PAIR, BLK), 0)
        return (row >= HEAD_DIM) if hh else (row < HEAD_DIM)

    def select_blocks(q_pair, n_past):
        jidx = lax.broadcasted_iota(jnp.int32, (nblk, BLK), 0)
        q_t = q_pair.astype(f32).T
        gates = _dot(kmean_ref[...], q_t.astype(bf))
        q_augs = []
        for hh in range(2):
            gate = gates[2 * hh * nblk:(2 * hh + 1) * nblk] + gates[(2 * hh + 1) * nblk:(2 * hh + 2) * nblk]
            gate = jnp.where(jidx < n_past, gate, NEG_INF)
            penalty = jnp.full((nblk, BLK), MOBA_PENALTY, f32)
            for _ in range(MOBA_TOPK):
                gmax = jnp.max(gate, axis=0, keepdims=True)
                first = jnp.min(jnp.where(gate == gmax, jidx, nblk), axis=0, keepdims=True)
                pick = (jidx == first) & (gmax > NEG_INF)
                penalty = jnp.where(pick, 0.0, penalty)
                gate = jnp.where(pick, NEG_INF, gate)
            pen0 = (HEAD_DIM if hh == 0 else 0) + MOBA_PEN_LANE
            pen_t = jnp.concatenate([jnp.zeros((pen0, BLK), f32), penalty,
                                     jnp.zeros((PAIR - pen0 - nblk, BLK), f32)], axis=0)
            q_augs.append(jnp.where(own_rows(hh), q_t, qfeat_ref[hh] + pen_t).astype(bf))
        return q_t.astype(bf), q_augs

    q_augs = [qaug_ref[0], qaug_ref[1]]
    q_t = qt_ref[...]
    last_group = nblk // MOBA_GROUP - 1

    def group_scores(dst_ref, g):
        j0 = jnp.clip(g, 0, last_group) * MOBA_GROUP
        rows = pl.ds(pl.multiple_of(j0 * BLK, MOBA_GROUP * BLK), MOBA_GROUP * BLK)
        tile_max = []
        for hh in range(2):
            k_g = kaug_ref[hh, rows, :]
            for i in range(MOBA_GROUP):
                s = _dot(k_g[i * BLK:(i + 1) * BLK], q_augs[hh])
                dst_ref[hh, i * BLK:(i + 1) * BLK, :] = s
                tile_max.append(jnp.max(s, axis=0, keepdims=True))
        return tuple(tile_max)

    def group_probs(s_ref, p_ref, g, tile_max, carry):
        j0 = jnp.clip(g, 0, last_group) * MOBA_GROUP
        new = []
        for hh in range(2):
            m, _, l, acc = carry[4 * hh:4 * hh + 4]
            block_shift = slopes_ref[2 * hp + hh] * (LOG2E * BLK)
            shifts = [block_shift * (n - j0 - i).astype(f32) for i in range(MOBA_GROUP)]
            m_new, a = _softmax_probs(s_ref.at[hh], p_ref.at[hh],
                                      tile_max[hh * MOBA_GROUP:(hh + 1) * MOBA_GROUP], shifts, m)
            new += [m_new, a, l, acc]
        return tuple(new)

    def group_values(p_ref, g, own, carry):
        j0 = jnp.clip(g, 0, last_group) * MOBA_GROUP
        blocks = [jnp.where(own, n, j0 + i) for i in range(MOBA_GROUP)]
        new = []
        for hh in range(2):
            m, a, l, acc = carry[4 * hh:4 * hh + 4]
            v_ts = [vt_ref[j][hh * HEAD_DIM:(hh + 1) * HEAD_DIM] for j in blocks]
            new += [m, a, *_weighted_values(p_ref.at[hh], v_ts, a, l, acc)]
        return tuple(new)

    n_groups = (n + MOBA_GROUP - 1) // MOBA_GROUP
    first_max = group_scores(sa_ref, 0)
    k_own = k_ref[pl.ds(pl.multiple_of(n * BLK, BLK), BLK), :]
    state = []
    for hh in range(2):
        s_own = _dot(k_own, jnp.where(own_rows(hh), q_t, jnp.zeros_like(q_t))) + bias_ref[hh]
        sown_ref[hh] = s_own
        m, a = _softmax_probs(sown_ref.at[hh], pb_ref.at[hh], [jnp.max(s_own, axis=0, keepdims=True)],
                              [0.0], jnp.full((1, BLK), NEG_INF, f32))
        pb_ref[hh, BLK:, :] = jnp.zeros((pb_ref.shape[1] - BLK, BLK), pb_ref.dtype)
        state += [m, a, jnp.zeros((1, BLK), f32), jnp.zeros((HEAD_DIM, BLK), f32)]

    next_t, next_augs = select_blocks(qnext_ref[...], n + 1)
    qt_ref[...] = next_t
    for hh in range(2):
        qaug_ref[hh] = next_augs[hh]

    def body(t, carry):
        carry, max_a = carry[:8], carry[8:]
        probs = group_probs(sa_ref, pa_ref, 2 * t, max_a, carry)
        pending = group_values(pb_ref, 2 * t - 1, t == 0, carry)
        carry = tuple(pending[i] if i % 4 >= 2 else probs[i] for i in range(8))
        max_b = group_scores(sb_ref, 2 * t + 1)
        carry = group_values(pa_ref, 2 * t, False, carry)
        max_a = group_scores(sa_ref, 2 * t + 2)
        return group_probs(sb_ref, pb_ref, 2 * t + 1, max_b, carry) + max_a

    pairs = n_groups // 2
    state = lax.fori_loop(0, pairs, body, tuple(state) + first_max)

    def flush_pending(args):
        return group_values(pb_ref, 2 * pairs - 1, pairs == 0, args[:8])

    def last_group_then_flush(args):
        carry, max_a = args[:8], args[8:]
        probs = group_probs(sa_ref, pa_ref, 2 * pairs, max_a, carry)
        pending = group_values(pb_ref, 2 * pairs - 1, pairs == 0, carry)
        carry = tuple(pending[i] if i % 4 >= 2 else probs[i] for i in range(8))
        return group_values(pa_ref, 2 * pairs, False, carry)

    state = lax.cond(n_groups % 2 == 1, last_group_then_flush, flush_pending, state)
    outs = [state[4 * hh + 3] / state[4 * hh + 2] for hh in range(2)]
    o_ref[...] = jnp.concatenate(outs, axis=0).T.astype(o_ref.dtype)


def _moba(qk_c, vt, slopes, batch, seq):
    nblk = seq // BLK
    npair = C_QKV // PAIR
    assert MOBA_PEN_LANE + nblk <= HEAD_DIM and nblk % (2 * MOBA_GROUP) == 0
    smem = pl.BlockSpec(memory_space=pltpu.SMEM)
    return pl.pallas_call(
        _moba_kernel,
        grid=(batch, npair, nblk),
        in_specs=[
            smem,
            pl.BlockSpec((BLK, PAIR), lambda b, hp, n: (b * nblk + n, hp)),
            pl.BlockSpec((BLK, PAIR), lambda b, hp, n: (b * nblk + jnp.minimum(n + 1, nblk - 1), hp)),
            pl.BlockSpec((seq, PAIR), lambda b, hp, n: (b, npair + hp)),
            pl.BlockSpec((nblk, PAIR, BLK), lambda b, hp, n: (b, hp, 0)),
        ],
        out_specs=pl.BlockSpec((BLK, PAIR), lambda b, hp, n: (b * nblk + n, hp)),
        out_shape=jax.ShapeDtypeStruct((batch * seq, C_QKV), jnp.bfloat16),
        scratch_shapes=[
            pltpu.VMEM((4 * nblk, PAIR), jnp.bfloat16),
            pltpu.VMEM((2, BLK, BLK), jnp.float32),
            pltpu.VMEM((2, seq, PAIR), jnp.bfloat16),
            pltpu.VMEM((2, PAIR, BLK), jnp.float32),
            pltpu.VMEM((2, PAIR, BLK), jnp.bfloat16),
            pltpu.VMEM((PAIR, BLK), jnp.bfloat16),
            pltpu.VMEM((2, BLK, BLK), jnp.float32),
            pltpu.VMEM((2, MOBA_GROUP * BLK, BLK), jnp.float32),
            pltpu.VMEM((2, MOBA_GROUP * BLK, BLK), jnp.float32),
            pltpu.VMEM((2, MOBA_GROUP * BLK, BLK), jnp.bfloat16),
            pltpu.VMEM((2, MOBA_GROUP * BLK, BLK), jnp.bfloat16),
        ],
        compiler_params=pltpu.CompilerParams(
            dimension_semantics=("arbitrary", "arbitrary", "arbitrary"), vmem_limit_bytes=VMEM_LIMIT),
        name="moba",
    )(slopes, qk_c, qk_c, qk_c, vt)


def _merge_kernel(x_ref, ya_ref, yc_ref, hbc_ref, halo_ref, wg_ref, convw_ref,
                  wa_ref, wb_ref, wc_ref, wo_ref, g_ref, b_ref, o_ref, u_ref, *, tiles_per_seq):
    i = pl.program_id(0)
    f32 = jnp.float32
    xb = x_ref[...].astype(jnp.bfloat16)
    gate_logits = [_dot(xb, wg_ref[:, k * D_MODEL:(k + 1) * D_MODEL]) for k in range(3)]

    def gated_input(ref):
        return ref[:, 2 * SC_WIDTH:].astype(f32) * ref[:, :SC_WIDTH].astype(f32)

    hist = jnp.where(i % tiles_per_seq == 0, 0.0, gated_input(halo_ref))
    u_ref[:HALO] = hist
    u_ref[HALO:] = gated_input(hbc_ref)
    conv = (u_ref[HALO - 2:HALO - 2 + ROW_TILE] * convw_ref[0:1, :]
            + u_ref[HALO - 1:HALO - 1 + ROW_TILE] * convw_ref[1:2, :]
            + u_ref[HALO:] * convw_ref[2:3, :])
    y_b = hbc_ref[:, SC_WIDTH:2 * SC_WIDTH].astype(f32) * conv

    merged = (jax.nn.sigmoid(gate_logits[0]) * _dot(ya_ref[...], wa_ref[...])
              + jax.nn.sigmoid(gate_logits[1]) * _dot(y_b.astype(jnp.bfloat16), wb_ref[...])
              + jax.nn.sigmoid(gate_logits[2]) * _dot(yc_ref[...], wc_ref[...]))
    mix = _dot(merged.astype(jnp.bfloat16), wo_ref[...])
    o_ref[...] = _layer_norm(ALPHA * x_ref[...] + mix, g_ref[...], b_ref[...])


def _merge(x2d, y_a, y_c, hbc, w_gates, conv_w, w_a, w_b, w_c, w_o, ln_g, ln_b, seq):
    n = x2d.shape[0]
    row = lambda i: (i, 0)
    const = lambda i: (0, 0)
    halo_blocks = ROW_TILE // HALO
    weight = lambda shape: pl.BlockSpec(shape, const, pipeline_mode=pl.Buffered(1))
    return pl.pallas_call(
        functools.partial(_merge_kernel, tiles_per_seq=seq // ROW_TILE),
        grid=(n // ROW_TILE,),
        in_specs=[
            pl.BlockSpec((ROW_TILE, D_MODEL), row),
            pl.BlockSpec((ROW_TILE, A_Q), row),
            pl.BlockSpec((ROW_TILE, C_QKV), row),
            pl.BlockSpec((ROW_TILE, HBC_W), row),
            pl.BlockSpec((HALO, HBC_W), lambda i: (jnp.maximum(i * halo_blocks - 1, 0), 0)),
            weight((D_MODEL, GATES_W)),
            weight((3, SC_WIDTH)),
            weight((A_Q, D_MODEL)), weight((SC_WIDTH, D_MODEL)), weight((C_QKV, D_MODEL)),
            weight((D_MODEL, D_MODEL)),
            weight((1, D_MODEL)), weight((1, D_MODEL)),
        ],
        out_specs=pl.BlockSpec((ROW_TILE, D_MODEL), row),
        out_shape=jax.ShapeDtypeStruct((n, D_MODEL), jnp.float32),
        scratch_shapes=[pltpu.VMEM((ROW_TILE + HALO, SC_WIDTH), jnp.float32)],
        compiler_params=pltpu.CompilerParams(
            dimension_semantics=("arbitrary",), vmem_limit_bytes=VMEM_LIMIT),
        name="merge",
    )(x2d, y_a, y_c, hbc, hbc, w_gates, conv_w, w_a, w_b, w_c, w_o, ln_g, ln_b)


def _ffn_kernel(x_ref, wg_ref, wu_ref, wd_ref, g_ref, b_ref, o_ref):
    x = x_ref[...]
    xb = x.astype(jnp.bfloat16)
    gate = _dot(xb, wg_ref[...])
    up = _dot(xb, wu_ref[...])
    act = (gate * jax.nn.sigmoid(gate) * up).astype(jnp.bfloat16)
    ffn = _dot(act, wd_ref[...])
    o_ref[...] = _layer_norm(ALPHA * x + ffn, g_ref[...], b_ref[...])


def _ffn(x2d, w_gate, w_up, w_down, ln_g, ln_b):
    n = x2d.shape[0]
    row = lambda i: (i, 0)
    const = lambda i: (0, 0)
    weight = lambda shape: pl.BlockSpec(shape, const, pipeline_mode=pl.Buffered(1))
    return pl.pallas_call(
        _ffn_kernel,
        grid=(n // ROW_TILE,),
        in_specs=[
            pl.BlockSpec((ROW_TILE, D_MODEL), row),
            weight((D_MODEL, FFN_HIDDEN)), weight((D_MODEL, FFN_HIDDEN)),
            weight((FFN_HIDDEN, D_MODEL)),
            weight((1, D_MODEL)), weight((1, D_MODEL)),
        ],
        out_specs=pl.BlockSpec((ROW_TILE, D_MODEL), row),
        out_shape=jax.ShapeDtypeStruct((n, D_MODEL), jnp.float32),
        compiler_params=pltpu.CompilerParams(
            dimension_semantics=("arbitrary",), vmem_limit_bytes=VMEM_LIMIT),
        name="ffn",
    )(x2d, w_gate, w_up, w_down, ln_g, ln_b)


def _alibi_slopes():
    i = jnp.arange(N_ALIBI_HEADS, dtype=jnp.float32)
    s = jnp.exp2(-8.0 * (i + 1.0) / N_ALIBI_HEADS)
    return s[:SWA_HEADS], s[SWA_HEADS:]


def _split_w_in(w_in):
    sizes = (A_Q, A_KV, A_KV, SC_WIDTH, SC_WIDTH, SC_WIDTH, C_QKV, C_QKV, C_QKV,
             D_MODEL, D_MODEL, D_MODEL)
    a_q, a_k, a_v, b_h, b_b, b_c, c_q, c_k, c_v, g_a, g_b, g_c = jnp.split(
        w_in, np.cumsum(sizes)[:-1].tolist(), axis=1)
    order = jnp.asarray(SWA_HEAD_ORDER)
    a_q = a_q.reshape(D_MODEL, SWA_HEADS, HEAD_DIM)[:, order].reshape(D_MODEL, A_Q)
    w_main = jnp.concatenate([a_q * Q_SCALE, a_k, c_q * Q_SCALE, c_k, b_h, b_b, b_c], axis=1)
    w_vt = jnp.concatenate([c_v, a_v], axis=1).T
    w_gates = jnp.concatenate([g_a, g_b, g_c], axis=1)
    return w_main.astype(jnp.bfloat16), w_vt.astype(jnp.bfloat16), w_gates.astype(jnp.bfloat16)


def kernel(x, w_in, attn_sinks, conv_w, w_branch_a, w_branch_b, w_branch_c, w_out,
           ln1_g, ln1_b, w_ffn_gate, w_ffn_up, w_ffn_down, ln2_g, ln2_b):
    batch, seq, d = x.shape
    assert d == D_MODEL and seq % ROW_TILE == 0 and seq % BLK == 0
    bf = jnp.bfloat16
    slopes_a, slopes_c = _alibi_slopes()
    order = jnp.asarray(SWA_HEAD_ORDER)
    h = x.reshape(batch * seq, d)
    for l in range(w_in.shape[0]):
        w_main, w_vt, w_gates = _split_w_in(w_in[l])
        w_a = w_branch_a[l].reshape(SWA_HEADS, HEAD_DIM, d)[order].reshape(A_Q, d).astype(bf)
        qk_a, qk_c, hbc, vt = _project(h, w_main, w_vt)
        y_a = _swa(qk_a, vt, slopes_a, attn_sinks[l].astype(jnp.float32), batch, seq)
        y_c = _moba(qk_c, vt, slopes_c, batch, seq)
        h = _merge(h, y_a, y_c, hbc, w_gates, conv_w[l], w_a, w_branch_b[l].astype(bf),
                   w_branch_c[l].astype(bf), w_out[l].astype(bf),
                   ln1_g[l].reshape(1, d), ln1_b[l].reshape(1, d), seq)
        h = _ffn(h, w_ffn_gate[l].astype(bf), w_ffn_up[l].astype(bf), w_ffn_down[l].astype(bf),
                 ln2_g[l].reshape(1, d), ln2_b[l].reshape(1, d))
    return h.reshape(batch, seq, d)
```
